```python
import jax, jax.numpy as jnp
from jax import lax
import numpy as np

D_MODEL = 4096
BATCH = 8
SEQ = 2048
DEPTH = 2

HEAD_DIM = 128
N_HEADS = D_MODEL // HEAD_DIM
N_KV_HEADS_B = max(1, N_HEADS // 4)
GROUP_B = N_HEADS // N_KV_HEADS_B
D_FF = ((8 * D_MODEL // 3 + 255) // 256) * 256
CONV_WIDTH = 3
BLOCK_Q = 128
N_A_LAYERS = DEPTH // 2
N_B_LAYERS = DEPTH - N_A_LAYERS
RMS_EPS = 1e-6

kernel_name = "yoco_stickbreak_fox_convglu"


def rms_norm(x, g):
    xf = x.astype(jnp.float32)
    y = xf * lax.rsqrt(jnp.mean(xf * xf, axis=-1, keepdims=True) + RMS_EPS)
    return (y * g.astype(jnp.float32)).astype(x.dtype)


def to_heads(t, n_heads):
    b, s, _ = t.shape
    return t.reshape(b, s, n_heads, HEAD_DIM).transpose(0, 2, 1, 3)


def merge_heads(t):
    b, n, s, dh = t.shape
    return t.transpose(0, 2, 1, 3).reshape(b, s, n * dh)


def stick_breaking_attention(q, k, v):
    seq = q.shape[2]
    scale = HEAD_DIM ** -0.5
    outs = []
    for i in range(seq // BLOCK_Q):
        q0 = i * BLOCK_Q
        kend = q0 + BLOCK_Q
        z = jnp.einsum('bhqd,bhkd->bhqk', q[:, :, q0:kend], k[:, :, :kend]).astype(jnp.float32) * scale
        t_pos = q0 + jnp.arange(BLOCK_Q)[:, None]
        s_pos = jnp.arange(kend)[None, :]
        strict = s_pos < t_pos
        log_1mb = jnp.where(strict, jax.nn.log_sigmoid(-z), 0.0)
        between = lax.cumsum(log_1mb, axis=3, reverse=True) - log_1mb
        w = jnp.where(strict, jnp.exp(jax.nn.log_sigmoid(z) + between), 0.0)
        outs.append(jnp.einsum('bhqk,bhkd->bhqd', w.astype(v.dtype), v[:, :, :kend]))
    return jnp.concatenate(outs, axis=2)


def forgetting_attention(q, k, v, c):
    seq = q.shape[3]
    scale = HEAD_DIM ** -0.5
    outs = []
    for i in range(seq // BLOCK_Q):
        q0 = i * BLOCK_Q
        kend = q0 + BLOCK_Q
        logits = jnp.einsum('bhgqd,bhkd->bhgqk', q[:, :, :, q0:kend], k[:, :, :kend]).astype(jnp.float32) * scale
        logits = logits + c[:, :, :, q0:kend, None] - c[:, :, :, None, :kend]
        t_pos = q0 + jnp.arange(BLOCK_Q)[:, None]
        s_pos = jnp.arange(kend)[None, :]
        logits = jnp.where(s_pos <= t_pos, logits, -jnp.inf)
        p = jax.nn.softmax(logits, axis=-1)
        outs.append(jnp.einsum('bhgqk,bhkd->bhgqd', p.astype(v.dtype), v[:, :, :kend]))
    return jnp.concatenate(outs, axis=3)


def conv_glu_ffn(xn, w_up, w_conv, b_conv, w_down):
    seq = xn.shape[1]
    gate, val = jnp.split(xn @ w_up, 2, axis=-1)
    gp = jnp.pad(gate, ((0, 0), (CONV_WIDTH - 1, 0), (0, 0)))
    conv = b_conv + sum(w_conv[j] * gp[:, j:j + seq] for j in range(CONV_WIDTH))
    return (jax.nn.silu(conv) * val) @ w_down


def setup_inputs(seed: int = 0) -> dict:
    key = jax.random.key(seed)
    ks = jax.random.split(key, 24)
    D, F = D_MODEL, D_FF
    kv_width = 2 * N_KV_HEADS_B * HEAD_DIM

    def nrm(k, shape, scale):
        return jax.random.normal(k, shape, jnp.float32) * scale

    def gain(k, shape):
        return 1.0 + 0.02 * jax.random.normal(k, shape, jnp.float32)

    nA, nB = N_A_LAYERS, N_B_LAYERS
    return {
        "x": nrm(ks[0], (BATCH, SEQ, D), 1.0),
        "a_attn_norm": gain(ks[1], (nA, D)),
        "a_w_qkv": nrm(ks[2], (nA, D, 3 * D), D ** -0.5),
        "a_w_o": nrm(ks[3], (nA, D, D), D ** -0.5),
        "a_ffn_norm": gain(ks[4], (nA, D)),
        "a_w_up": nrm(ks[5], (nA, D, 2 * F), D ** -0.5),
        "a_w_conv": nrm(ks[6], (nA, CONV_WIDTH, F), CONV_WIDTH ** -0.5),
        "a_b_conv": nrm(ks[7], (nA, F), 0.02),
        "a_w_down": nrm(ks[8], (nA, F, D), F ** -0.5),
        "kv_norm": gain(ks[9], (D,)),
        "w_kv": nrm(ks[10], (D, kv_width), D ** -0.5),
        "w_f": nrm(ks[11], (D, N_HEADS), 0.5 * D ** -0.5),
        "b_f": 3.0 + 0.5 * jax.random.normal(ks[12], (N_HEADS,), jnp.float32),
        "b_attn_norm": gain(ks[13], (nB, D)),
        "b_w_q": nrm(ks[14], (nB, D, D), D ** -0.5),
        "b_w_o": nrm(ks[15], (nB, D, D), D ** -0.5),
        "b_ffn_norm": gain(ks[16], (nB, D)),
        "b_w_up": nrm(ks[17], (nB, D, 2 * F), D ** -0.5),
        "b_w_conv": nrm(ks[18], (nB, CONV_WIDTH, F), CONV_WIDTH ** -0.5),
        "b_b_conv": nrm(ks[19], (nB, F), 0.02),
        "b_w_down": nrm(ks[20], (nB, F, D), F ** -0.5),
        "final_norm": gain(ks[21], (D,)),
    }


def reference(x, a_attn_norm, a_w_qkv, a_w_o, a_ffn_norm, a_w_up, a_w_conv, a_b_conv, a_w_down,
              kv_norm, w_kv, w_f, b_f,
              b_attn_norm, b_w_q, b_w_o, b_ffn_norm, b_w_up, b_w_conv, b_b_conv, b_w_down,
              final_norm):
    bsz, seq, _ = x.shape
    h = x
    for l in range(DEPTH):
        if l < N_A_LAYERS:
            i = l
            xn = rms_norm(h, a_attn_norm[i])
            q, k, v = jnp.split(xn @ a_w_qkv[i], 3, axis=-1)
            o = stick_breaking_attention(to_heads(q, N_HEADS), to_heads(k, N_HEADS), to_heads(v, N_HEADS))
            h = h + merge_heads(o) @ a_w_o[i]
            h = h + conv_glu_ffn(rms_norm(h, a_ffn_norm[i]), a_w_up[i], a_w_conv[i], a_b_conv[i], a_w_down[i])
        else:
            j = l - N_A_LAYERS
            if l == N_A_LAYERS:
                kv_in = rms_norm(h, kv_norm)
                k_s, v_s = jnp.split(kv_in @ w_kv, 2, axis=-1)
                k_s = to_heads(k_s, N_KV_HEADS_B)
                v_s = to_heads(v_s, N_KV_HEADS_B)
                log_f = jax.nn.log_sigmoid((kv_in @ w_f + b_f).astype(jnp.float32))
                c_s = jnp.cumsum(log_f, axis=1).transpose(0, 2, 1).reshape(bsz, N_KV_HEADS_B, GROUP_B, seq)
            xn = rms_norm(h, b_attn_norm[j])
            q = to_heads(xn @ b_w_q[j], N_HEADS).reshape(bsz, N_KV_HEADS_B, GROUP_B, seq, HEAD_DIM)
            o = forgetting_attention(q, k_s, v_s, c_s).reshape(bsz, N_HEADS, seq, HEAD_DIM)
            h = h + merge_heads(o) @ b_w_o[j]
            h = h + conv_glu_ffn(rms_norm(h, b_ffn_norm[j]), b_w_up[j], b_w_conv[j], b_b_conv[j], b_w_down[j])
    return rms_norm(h, final_norm)
```

```python
import functools

import jax
import jax.numpy as jnp
from jax import lax
from jax.experimental import pallas as pl
from jax.experimental.pallas import tpu as pltpu

HEAD_DIM = 128
CONV_WIDTH = 3
RMS_EPS = 1e-6

V7X_LANES = 128
V7X_SUBLANES = 8
V7X_MXU_DIM = 256
V7X_VMEM_LIMIT_BYTES = 56 * 1024 * 1024

F32 = jnp.float32
BF16 = jnp.bfloat16

NORM_ROW_CHUNK = 32
MASK_VALUE = -1e30


def _params(*semantics):
    return pltpu.CompilerParams(dimension_semantics=semantics,
                                vmem_limit_bytes=V7X_VMEM_LIMIT_BYTES)


def _pick_tile(n, target, quantum):
    if n <= target:
        return n
    t = (target // quantum) * quantum
    while t >= quantum:
        if n % t == 0:
            return t
        t -= quantum
    raise ValueError(f"no tile for {n}")


def _rms_to_scratch(h_ref, g_ref, xn_ref):
    tm = h_ref.shape[0]

    def body(c, carry):
        r0 = pl.multiple_of(c * NORM_ROW_CHUNK, NORM_ROW_CHUNK)
        x = h_ref[pl.ds(r0, NORM_ROW_CHUNK), :]
        ms = jnp.mean(x * x, axis=-1, keepdims=True)
        y = x * lax.rsqrt(ms + RMS_EPS) * g_ref[...]
        xn_ref[pl.ds(r0, NORM_ROW_CHUNK), :] = y.astype(xn_ref.dtype)
        return carry

    lax.fori_loop(0, tm // NORM_ROW_CHUNK, body, 0)


def _norm_matmul_kernel(h_ref, g_ref, w_ref, o_ref, xn_ref):
    @pl.when(pl.program_id(1) == 0)
    def _():
        _rms_to_scratch(h_ref, g_ref, xn_ref)

    o_ref[...] = jnp.dot(xn_ref[...], w_ref[...],
                         preferred_element_type=F32).astype(o_ref.dtype)


def norm_matmul(h, gain, w, *, tm_target=1024, tn_target=512):
    m, d = h.shape
    n = w.shape[1]
    tm = _pick_tile(m, tm_target, NORM_ROW_CHUNK)
    tn = _pick_tile(n, tn_target, V7X_LANES)
    return pl.pallas_call(
        _norm_matmul_kernel,
        grid=(m // tm, n // tn),
        in_specs=[
            pl.BlockSpec((tm, d), lambda i, j: (i, 0)),
            pl.BlockSpec((1, d), lambda i, j: (0, 0)),
            pl.BlockSpec((d, tn), lambda i, j: (0, j)),
        ],
        out_specs=pl.BlockSpec((tm, tn), lambda i, j: (i, j)),
        out_shape=jax.ShapeDtypeStruct((m, n), BF16),
        scratch_shapes=[pltpu.VMEM((tm, d), BF16)],
        compiler_params=_params("parallel", "arbitrary"),
        name="norm_matmul",
    )(h, gain.reshape(1, d), w)


def _kv_proj_kernel(h_ref, g_ref, w_ref, wf_ref, bf_ref, o_ref, lf_ref, xn_ref):
    @pl.when(pl.program_id(1) == 0)
    def _():
        _rms_to_scratch(h_ref, g_ref, xn_ref)
        pre = jnp.dot(xn_ref[...], wf_ref[...], preferred_element_type=F32) + bf_ref[...]
        lf_ref[...] = -(jnp.maximum(-pre, 0.0) + jnp.log(1.0 + jnp.exp(-jnp.abs(pre))))

    o_ref[...] = jnp.dot(xn_ref[...], w_ref[...],
                         preferred_element_type=F32).astype(o_ref.dtype)


def kv_proj(h, gain, w_kv, wf_pad, bf_pad, *, tm_target=1024, tn_target=512):
    m, d = h.shape
    n = w_kv.shape[1]
    nf = wf_pad.shape[1]
    tm = _pick_tile(m, tm_target, NORM_ROW_CHUNK)
    tn = _pick_tile(n, tn_target, V7X_LANES)
    return pl.pallas_call(
        _kv_proj_kernel,
        grid=(m // tm, n // tn),
        in_specs=[
            pl.BlockSpec((tm, d), lambda i, j: (i, 0)),
            pl.BlockSpec((1, d), lambda i, j: (0, 0)),
            pl.BlockSpec((d, tn), lambda i, j: (0, j)),
            pl.BlockSpec((d, nf), lambda i, j: (0, 0)),
            pl.BlockSpec((1, nf), lambda i, j: (0, 0)),
        ],
        out_specs=[
            pl.BlockSpec((tm, tn), lambda i, j: (i, j)),
            pl.BlockSpec((tm, nf), lambda i, j: (i, 0)),
        ],
        out_shape=[
            jax.ShapeDtypeStruct((m, n), BF16),
            jax.ShapeDtypeStruct((m, nf), F32),
        ],
        scratch_shapes=[pltpu.VMEM((tm, d), BF16)],
        compiler_params=_params("parallel", "arbitrary"),
        name="kv_proj",
    )(h, gain.reshape(1, d), w_kv, wf_pad, bf_pad)


def _split3(x):
    hi = x.astype(BF16)
    r1 = x - hi.astype(F32)
    mid = r1.astype(BF16)
    lo = (r1 - mid.astype(F32)).astype(BF16)
    return hi, mid, lo


def _cumsum_kernel(x_ref, tri_ref, c_ref, run_ref):
    blk = tri_ref.shape[0]
    nblk = x_ref.shape[0] // blk
    run_ref[...] = jnp.zeros_like(run_ref)

    def body(n, carry):
        r0 = pl.multiple_of(n * blk, blk)
        hi, mid, lo = _split3(x_ref[pl.ds(r0, blk), :])
        tri = tri_ref[...]
        incl = (jnp.dot(tri, hi, preferred_element_type=F32)
                + jnp.dot(tri, mid, preferred_element_type=F32)
                + jnp.dot(tri, lo, preferred_element_type=F32)
                + run_ref[0:1, :])
        c_ref[pl.ds(r0, blk), :] = incl
        run_ref[0:1, :] = incl[blk - 1:blk, :]
        return carry

    lax.fori_loop(0, nblk, body, 0)


def cumsum_positions(x):
    b, s, c = x.shape
    blk = V7X_LANES
    tri = (lax.broadcasted_iota(jnp.int32, (blk, blk), 1)
           <= lax.broadcasted_iota(jnp.int32, (blk, blk), 0)).astype(BF16)
    return pl.pallas_call(
        _cumsum_kernel,
        grid=(b,),
        in_specs=[
            pl.BlockSpec((None, s, c), lambda i: (i, 0, 0)),
            pl.BlockSpec((blk, blk), lambda i: (0, 0)),
        ],
        out_specs=pl.BlockSpec((None, s, c), lambda i: (i, 0, 0)),
        out_shape=jax.ShapeDtypeStruct((b, s, c), F32),
        scratch_shapes=[pltpu.VMEM((V7X_SUBLANES, c), F32)],
        compiler_params=_params("parallel"),
        name="cumsum_positions",
    )(x, tri)


def _matmul_residual_kernel(a_ref, w_ref, r_ref, o_ref):
    o_ref[...] = r_ref[...] + jnp.dot(a_ref[...], w_ref[...], preferred_element_type=F32)


def matmul_residual(a, w, res, *, tm_target, tn_target):
    m, k = a.shape
    n = w.shape[1]
    tm = _pick_tile(m, tm_target, 2 * V7X_SUBLANES)
    tn = _pick_tile(n, tn_target, V7X_LANES)
    return pl.pallas_call(
        _matmul_residual_kernel,
        grid=(m // tm, n // tn),
        in_specs=[
            pl.BlockSpec((tm, k), lambda i, j: (i, 0)),
            pl.BlockSpec((k, tn), lambda i, j: (0, j)),
            pl.BlockSpec((tm, tn), lambda i, j: (i, j)),
        ],
        out_specs=pl.BlockSpec((tm, tn), lambda i, j: (i, j)),
        out_shape=jax.ShapeDtypeStruct((m, n), F32),
        compiler_params=_params("parallel", "parallel"),
        name="matmul_residual",
    )(a, w, res)


def _ffn_up_kernel(h_ref, g_ref, wg_ref, wv_ref, wc_ref, bc_ref, o_ref,
                   xn_ref, gbuf_ref, tail_ref, *, tiles_per_seq):
    i = pl.program_id(0)
    j = pl.program_id(1)
    tm, tn = o_ref.shape
    halo = V7X_SUBLANES

    @pl.when(j == 0)
    def _():
        _rms_to_scratch(h_ref, g_ref, xn_ref)

    xn = xn_ref[...]
    gate = jnp.dot(xn, wg_ref[...], preferred_element_type=F32)
    val = jnp.dot(xn, wv_ref[...], preferred_element_type=F32)

    seq_start = (i % tiles_per_seq) == 0

    @pl.when(seq_start)
    def _():
        gbuf_ref[0:halo, :] = jnp.zeros((halo, tn), F32)

    @pl.when(jnp.logical_not(seq_start))
    def _():
        gbuf_ref[0:halo, :] = tail_ref[j]

    gbuf_ref[halo:halo + tm, :] = gate
    tail_ref[j] = gate[tm - halo:tm, :]

    g1 = gbuf_ref[halo - 1:halo - 1 + tm, :]
    g2 = gbuf_ref[halo - 2:halo - 2 + tm, :]
    wc = wc_ref[...]
    conv = bc_ref[...] + (wc[0:1, :] * g2 + wc[1:2, :] * g1 + wc[2:3, :] * gate)
    act = conv * (1.0 / (1.0 + jnp.exp(-conv))) * val
    o_ref[...] = act.astype(o_ref.dtype)


def ffn_up(h, gain, w_up, w_conv, b_conv, seq, *, tm_target=1024, tn_target=256):
    m, d = h.shape
    f = w_up.shape[1] // 2
    tm = _pick_tile(seq, tm_target, NORM_ROW_CHUNK)
    tn = _pick_tile(f, tn_target, V7X_LANES)
    nj = f // tn
    halo = V7X_SUBLANES
    assert tm >= halo and CONV_WIDTH - 1 <= halo
    return pl.pallas_call(
        functools.partial(_ffn_up_kernel, tiles_per_seq=seq // tm),
        grid=(m // tm, nj),
        in_specs=[
            pl.BlockSpec((tm, d), lambda i, j: (i, 0)),
            pl.BlockSpec((1, d), lambda i, j: (0, 0)),
            pl.BlockSpec((d, tn), lambda i, j: (0, j)),
            pl.BlockSpec((d, tn), lambda i, j: (0, j + nj)),
            pl.BlockSpec((CONV_WIDTH, tn), lambda i, j: (0, j)),
            pl.BlockSpec((1, tn), lambda i, j: (0, j)),
        ],
        out_specs=pl.BlockSpec((tm, tn), lambda i, j: (i, j)),
        out_shape=jax.ShapeDtypeStruct((m, f), BF16),
        scratch_shapes=[
            pltpu.VMEM((tm, d), BF16),
            pltpu.VMEM((tm + halo, tn), F32),
            pltpu.VMEM((nj, halo, tn), F32),
        ],
        compiler_params=_params("arbitrary", "arbitrary"),
        name="ffn_up",
    )(h, gain.reshape(1, d), w_up, w_up, w_conv, b_conv.reshape(1, f))


def _sb_attn_kernel(q_ref, k_ref, v_ref, u_ref, o_ref, acc_ref, run_ref, *, tq, tk, scale):
    seq = q_ref.shape[0]
    r = tq // tk

    def one_block(q, q0, kb, masked):
        k0 = pl.multiple_of(kb * tk, tk)
        kblk = k_ref[pl.ds(k0, tk), :]
        z = lax.dot_general(q, kblk, (((1,), (1,)), ((), ())),
                            preferred_element_type=F32) * scale
        lsm = -(jnp.maximum(z, 0.0) + jnp.log(1.0 + jnp.exp(-jnp.abs(z))))
        if masked:
            t_pos = q0 + lax.broadcasted_iota(jnp.int32, (tq, tk), 0)
            s_pos = k0 + lax.broadcasted_iota(jnp.int32, (tq, tk), 1)
            strict = s_pos < t_pos
            lsm = jnp.where(strict, lsm, 0.0)
        hi = lsm.astype(BF16)
        lo = (lsm - hi.astype(F32)).astype(BF16)
        u = u_ref[...]
        incl = (jnp.dot(hi, u, preferred_element_type=F32)
                + jnp.dot(lo, u, preferred_element_type=F32))
        w = jnp.exp(z + incl + run_ref[...])
        if masked:
            w = jnp.where(strict, w, 0.0)
        acc_ref[...] += jnp.dot(w.astype(BF16), v_ref[pl.ds(k0, tk), :],
                                preferred_element_type=F32)
        run_ref[...] += incl[:, 0:1]

    def q_body(qi, carry):
        q0 = pl.multiple_of(qi * tq, tq)
        q = q_ref[pl.ds(q0, tq), :]
        acc_ref[...] = jnp.zeros_like(acc_ref)
        run_ref[...] = jnp.zeros_like(run_ref)
        for d in range(r):
            one_block(q, q0, qi * r + (r - 1 - d), True)

        def kb_body(n, c):
            one_block(q, q0, qi * r - 1 - n, False)
            return c

        lax.fori_loop(0, qi * r, kb_body, 0)
        o_ref[pl.ds(q0, tq), :] = acc_ref[...].astype(o_ref.dtype)
        return carry

    lax.fori_loop(0, seq // tq, q_body, 0)


def stick_breaking_attention(qkv, n_heads, *, tq_target=512, tk=V7X_MXU_DIM):
    b, s, _ = qkv.shape
    tk = min(tk, s)
    tq = _pick_tile(s, tq_target, tk)
    u = (lax.broadcasted_iota(jnp.int32, (tk, tk), 0)
         >= lax.broadcasted_iota(jnp.int32, (tk, tk), 1)).astype(BF16)
    kern = functools.partial(_sb_attn_kernel, tq=tq, tk=tk, scale=HEAD_DIM ** -0.5)
    return pl.pallas_call(
        kern,
        grid=(b, n_heads),
        in_specs=[
            pl.BlockSpec((None, s, HEAD_DIM), lambda bi, h: (bi, 0, h)),
            pl.BlockSpec((None, s, HEAD_DIM), lambda bi, h: (bi, 0, n_heads + h)),
            pl.BlockSpec((None, s, HEAD_DIM), lambda bi, h: (bi, 0, 2 * n_heads + h)),
            pl.BlockSpec((tk, tk), lambda bi, h: (0, 0)),
        ],
        out_specs=pl.BlockSpec((None, s, HEAD_DIM), lambda bi, h: (bi, 0, h)),
        out_shape=jax.ShapeDtypeStruct((b, s, n_heads * HEAD_DIM), BF16),
        scratch_shapes=[
            pltpu.VMEM((tq, HEAD_DIM), F32),
            pltpu.VMEM((tq, 1), F32),
        ],
        compiler_params=_params("parallel", "parallel"),
        name="stick_breaking_attention",
    )(qkv, qkv, qkv, u)


def _fox_attn_kernel(q_ref, k_ref, v_ref, ccol_ref, crow_ref, o_ref,
                     acc_ref, m_ref, l_ref, *, tq, tk, group, scale):
    seq = k_ref.shape[0]
    r = tq // tk
    kvh = pl.program_id(1)

    def one_block(q, bias_t, q0, kb, masked):
        k0 = pl.multiple_of(kb * tk, tk)
        kblk = k_ref[pl.ds(k0, tk), :]
        s_all = lax.dot_general(q, kblk, (((1,), (1,)), ((), ())),
                                preferred_element_type=F32) * scale
        if masked:
            t_pos = q0 + lax.broadcasted_iota(jnp.int32, (tq, tk), 0)
            s_pos = k0 + lax.broadcasted_iota(jnp.int32, (tq, tk), 1)
            causal = s_pos <= t_pos
        p_parts = []
        for g in range(group):
            rows = slice(g * tq, (g + 1) * tq)
            logit = s_all[rows, :] + bias_t[g] - crow_ref[g:g + 1, pl.ds(k0, tk)]
            if masked:
                logit = jnp.where(causal, logit, MASK_VALUE)
            m_old = m_ref[rows, :]
            m_new = jnp.maximum(m_old, jnp.max(logit, axis=-1, keepdims=True))
            alpha = jnp.exp(m_old - m_new)
            p = jnp.exp(logit - m_new)
            l_ref[rows, :] = alpha * l_ref[rows, :] + jnp.sum(p, axis=-1, keepdims=True)
            acc_ref[rows, :] = alpha * acc_ref[rows, :]
            m_ref[rows, :] = m_new
            p_parts.append(p.astype(BF16))
        p_all = jnp.concatenate(p_parts, axis=0)
        acc_ref[...] += jnp.dot(p_all, v_ref[pl.ds(k0, tk), :], preferred_element_type=F32)

    def q_body(qi, carry):
        q0 = pl.multiple_of(qi * tq, tq)
        q = jnp.concatenate(
            [q_ref[pl.ds(q0, tq), g * HEAD_DIM:(g + 1) * HEAD_DIM] for g in range(group)], axis=0)
        cblk = ccol_ref[pl.ds(q0, tq), :]
        lane = lax.broadcasted_iota(jnp.int32, cblk.shape, 1)
        bias_t = [jnp.sum(jnp.where(lane == kvh * group + g, cblk, 0.0), axis=-1, keepdims=True)
                  for g in range(group)]
        acc_ref[...] = jnp.zeros_like(acc_ref)
        l_ref[...] = jnp.zeros_like(l_ref)
        m_ref[...] = jnp.full_like(m_ref, MASK_VALUE)
        for d in range(r):
            one_block(q, bias_t, q0, qi * r + d, True)

        def kb_body(kb, c):
            one_block(q, bias_t, q0, kb, False)
            return c

        lax.fori_loop(0, qi * r, kb_body, 0)
        out = acc_ref[...] / l_ref[...]
        for g in range(group):
            o_ref[pl.ds(q0, tq), g * HEAD_DIM:(g + 1) * HEAD_DIM] = (
                out[g * tq:(g + 1) * tq, :].astype(o_ref.dtype))
        return carry

    lax.fori_loop(0, seq // tq, q_body, 0)


def forgetting_attention(q, kv, c_col, c_row, n_kv_heads, *, tq_target=256, tk=V7X_MXU_DIM):
    b, s, dq = q.shape
    group = dq // HEAD_DIM // n_kv_heads
    tk = min(tk, s)
    tq = _pick_tile(s, tq_target, tk)
    cw = c_col.shape[2]
    kern = functools.partial(_fox_attn_kernel, tq=tq, tk=tk, group=group, scale=HEAD_DIM ** -0.5)
    return pl.pallas_call(
        kern,
        grid=(b, n_kv_heads),
        in_specs=[
            pl.BlockSpec((None, s, group * HEAD_DIM), lambda bi, h: (bi, 0, h)),
            pl.BlockSpec((None, s, HEAD_DIM), lambda bi, h: (bi, 0, h)),
            pl.BlockSpec((None, s, HEAD_DIM), lambda bi, h: (bi, 0, n_kv_heads + h)),
            pl.BlockSpec((None, s, cw), lambda bi, h: (bi, 0, 0)),
            pl.BlockSpec((None, None, group, s), lambda bi, h: (bi, h, 0, 0)),
        ],
        out_specs=pl.BlockSpec((None, s, group * HEAD_DIM), lambda bi, h: (bi, 0, h)),
        out_shape=jax.ShapeDtypeStruct((b, s, dq), BF16),
        scratch_shapes=[
            pltpu.VMEM((group * tq, HEAD_DIM), F32),
            pltpu.VMEM((group * tq, 1), F32),
            pltpu.VMEM((group * tq, 1), F32),
        ],
        compiler_params=_params("parallel", "parallel"),
        name="forgetting_attention",
    )(q, kv, kv, c_col, c_row)


def _rms_norm_kernel(h_ref, g_ref, o_ref):
    x = h_ref[...]
    ms = jnp.mean(x * x, axis=-1, keepdims=True)
    o_ref[...] = x * lax.rsqrt(ms + RMS_EPS) * g_ref[...]


def rms_norm(h, gain, *, tm_target=256):
    m, d = h.shape
    tm = _pick_tile(m, tm_target, V7X_SUBLANES)
    return pl.pallas_call(
        _rms_norm_kernel,
        grid=(m // tm,),
        in_specs=[pl.BlockSpec((tm, d), lambda i: (i, 0)),
                  pl.BlockSpec((1, d), lambda i: (0, 0))],
        out_specs=pl.BlockSpec((tm, d), lambda i: (i, 0)),
        out_shape=jax.ShapeDtypeStruct((m, d), F32),
        compiler_params=_params("parallel"),
        name="final_rms_norm",
    )(h, gain.reshape(1, d))


def _conv_glu_ffn(h, gain, w_up, w_conv, b_conv, w_down, seq):
    act = ffn_up(h, gain, w_up.astype(BF16), w_conv, b_conv, seq)
    return matmul_residual(act, w_down.astype(BF16), h, tm_target=512, tn_target=256)


def kernel(x, a_attn_norm, a_w_qkv, a_w_o, a_ffn_norm, a_w_up, a_w_conv, a_b_conv, a_w_down,
           kv_norm, w_kv, w_f, b_f,
           b_attn_norm, b_w_q, b_w_o, b_ffn_norm, b_w_up, b_w_conv, b_b_conv, b_w_down,
           final_norm):
    bsz, seq, d = x.shape
    m = bsz * seq
    n_heads = d // HEAD_DIM
    n_kv_heads = w_kv.shape[1] // (2 * HEAD_DIM)
    group = n_heads // n_kv_heads
    n_gates = w_f.shape[1]
    assert n_gates == n_heads and n_gates <= V7X_LANES

    h = x.reshape(m, d)

    for i in range(a_w_qkv.shape[0]):
        qkv = norm_matmul(h, a_attn_norm[i], a_w_qkv[i].astype(BF16))
        o = stick_breaking_attention(qkv.reshape(bsz, seq, 3 * d), n_heads)
        h = matmul_residual(o.reshape(m, d), a_w_o[i].astype(BF16), h, tm_target=1024, tn_target=512)
        h = _conv_glu_ffn(h, a_ffn_norm[i], a_w_up[i], a_w_conv[i], a_b_conv[i], a_w_down[i], seq)

    kv = c_col = c_row = None
    for j in range(b_w_q.shape[0]):
        if j == 0:
            wf_pad = jnp.pad(w_f, ((0, 0), (0, V7X_LANES - n_gates))).astype(BF16)
            bf_pad = jnp.pad(b_f, (0, V7X_LANES - n_gates)).reshape(1, V7X_LANES)
            kv, log_f = kv_proj(h, kv_norm, w_kv.astype(BF16), wf_pad, bf_pad)
            kv = kv.reshape(bsz, seq, 2 * n_kv_heads * HEAD_DIM)
            c_col = cumsum_positions(log_f.reshape(bsz, seq, V7X_LANES))
            c_row = (c_col[:, :, :n_gates].transpose(0, 2, 1)
                     .reshape(bsz, n_kv_heads, group, seq))
        q = norm_matmul(h, b_attn_norm[j], b_w_q[j].astype(BF16))
        o = forgetting_attention(q.reshape(bsz, seq, d), kv, c_col, c_row, n_kv_heads)
        h = matmul_residual(o.reshape(m, d), b_w_o[j].astype(BF16), h, tm_target=1024, tn_target=512)
        h = _conv_glu_ffn(h, b_ffn_norm[j], b_w_up[j], b_w_conv[j], b_b_conv[j], b_w_down[j], seq)

    return rms_norm(h, final_norm).reshape(bsz, seq, d)
```

```python
import functools

import jax
import jax.numpy as jnp
from jax import lax
from jax.experimental import pallas as pl
from jax.experimental.pallas import tpu as pltpu

HEAD_DIM = 128
CONV_WIDTH = 3
RMS_EPS = 1e-6

V7X_LANES = 128
V7X_SUBLANES = 8
V7X_MXU_DIM = 256
V7X_VMEM_LIMIT_BYTES = 56 * 1024 * 1024

F32 = jnp.float32
BF16 = jnp.bfloat16

NORM_ROW_CHUNK = 32
ATTN_ROW_CHUNK = 32
MASK_VALUE = -1e30
LOG2_E = 1.4426950408889634


def _params(*semantics):
    return pltpu.CompilerParams(dimension_semantics=semantics,
                                vmem_limit_bytes=V7X_VMEM_LIMIT_BYTES)


def _pick_tile(n, target, quantum):
    if n <= target:
        return n
    t = (target // quantum) * quantum
    while t >= quantum:
        if n % t == 0:
            return t
        t -= quantum
    raise ValueError(f"no tile for {n}")


def _rms_to_scratch(h_ref, g_ref, xn_ref):
    tm = h_ref.shape[0]

    def body(c, carry):
        r0 = pl.multiple_of(c * NORM_ROW_CHUNK, NORM_ROW_CHUNK)
        x = h_ref[pl.ds(r0, NORM_ROW_CHUNK), :]
        ms = jnp.mean(x * x, axis=-1, keepdims=True)
        y = x * lax.rsqrt(ms + RMS_EPS) * g_ref[...]
        xn_ref[pl.ds(r0, NORM_ROW_CHUNK), :] = y.astype(xn_ref.dtype)
        return carry

    lax.fori_loop(0, tm // NORM_ROW_CHUNK, body, 0, unroll=2)


def _norm_matmul_kernel(h_ref, g_ref, w_ref, o_ref, xn_ref):
    @pl.when(pl.program_id(1) == 0)
    def _():
        _rms_to_scratch(h_ref, g_ref, xn_ref)

    o_ref[...] = jnp.dot(xn_ref[...], w_ref[...],
                         preferred_element_type=F32).astype(o_ref.dtype)


def norm_matmul(h, gain, w, *, tm_target=1024, tn_target=512):
    m, d = h.shape
    n = w.shape[1]
    tm = _pick_tile(m, tm_target, NORM_ROW_CHUNK)
    tn = _pick_tile(n, tn_target, V7X_LANES)
    return pl.pallas_call(
        _norm_matmul_kernel,
        grid=(m // tm, n // tn),
        in_specs=[
            pl.BlockSpec((tm, d), lambda i, j: (i, 0)),
            pl.BlockSpec((1, d), lambda i, j: (0, 0)),
            pl.BlockSpec((d, tn), lambda i, j: (0, j)),
        ],
        out_specs=pl.BlockSpec((tm, tn), lambda i, j: (i, j)),
        out_shape=jax.ShapeDtypeStruct((m, n), BF16),
        scratch_shapes=[pltpu.VMEM((tm, d), BF16)],
        compiler_params=_params("parallel", "arbitrary"),
        name="norm_matmul",
    )(h, gain.reshape(1, d), w)


def _kv_proj_kernel(h_ref, g_ref, w_ref, wf_ref, bf_ref, o_ref, lf_ref, xn_ref):
    @pl.when(pl.program_id(1) == 0)
    def _():
        _rms_to_scratch(h_ref, g_ref, xn_ref)
        pre = jnp.dot(xn_ref[...], wf_ref[...], preferred_element_type=F32) + bf_ref[...]
        lf_ref[...] = -(jnp.maximum(-pre, 0.0) + jnp.log(1.0 + jnp.exp(-jnp.abs(pre))))

    o_ref[...] = jnp.dot(xn_ref[...], w_ref[...],
                         preferred_element_type=F32).astype(o_ref.dtype)


def kv_proj(h, gain, w_kv, wf_pad, bf_pad, *, tm_target=1024, tn_target=512):
    m, d = h.shape
    n = w_kv.shape[1]
    nf = wf_pad.shape[1]
    tm = _pick_tile(m, tm_target, NORM_ROW_CHUNK)
    tn = _pick_tile(n, tn_target, V7X_LANES)
    return pl.pallas_call(
        _kv_proj_kernel,
        grid=(m // tm, n // tn),
        in_specs=[
            pl.BlockSpec((tm, d), lambda i, j: (i, 0)),
            pl.BlockSpec((1, d), lambda i, j: (0, 0)),
            pl.BlockSpec((d, tn), lambda i, j: (0, j)),
            pl.BlockSpec((d, nf), lambda i, j: (0, 0)),
            pl.BlockSpec((1, nf), lambda i, j: (0, 0)),
        ],
        out_specs=[
            pl.BlockSpec((tm, tn), lambda i, j: (i, j)),
            pl.BlockSpec((tm, nf), lambda i, j: (i, 0)),
        ],
        out_shape=[
            jax.ShapeDtypeStruct((m, n), BF16),
            jax.ShapeDtypeStruct((m, nf), F32),
        ],
        scratch_shapes=[pltpu.VMEM((tm, d), BF16)],
        compiler_params=_params("parallel", "arbitrary"),
        name="kv_proj",
    )(h, gain.reshape(1, d), w_kv, wf_pad, bf_pad)


def _split3(x):
    hi = x.astype(BF16)
    r1 = x - hi.astype(F32)
    mid = r1.astype(BF16)
    lo = (r1 - mid.astype(F32)).astype(BF16)
    return hi, mid, lo


def _cumsum_kernel(x_ref, tri_ref, c_ref, run_ref):
    blk = tri_ref.shape[0]
    nblk = x_ref.shape[0] // blk
    run_ref[...] = jnp.zeros_like(run_ref)

    def body(n, carry):
        r0 = pl.multiple_of(n * blk, blk)
        hi, mid, lo = _split3(x_ref[pl.ds(r0, blk), :])
        tri = tri_ref[...]
        incl = (jnp.dot(tri, hi, preferred_element_type=F32)
                + jnp.dot(tri, mid, preferred_element_type=F32)
                + jnp.dot(tri, lo, preferred_element_type=F32)
                + run_ref[0:1, :])
        c_ref[pl.ds(r0, blk), :] = incl
        run_ref[0:1, :] = incl[blk - 1:blk, :]
        return carry

    lax.fori_loop(0, nblk, body, 0)


def cumsum_positions(x):
    b, s, c = x.shape
    blk = V7X_LANES
    tri = (lax.broadcasted_iota(jnp.int32, (blk, blk), 1)
           <= lax.broadcasted_iota(jnp.int32, (blk, blk), 0)).astype(BF16)
    return pl.pallas_call(
        _cumsum_kernel,
        grid=(b,),
        in_specs=[
            pl.BlockSpec((None, s, c), lambda i: (i, 0, 0)),
            pl.BlockSpec((blk, blk), lambda i: (0, 0)),
        ],
        out_specs=pl.BlockSpec((None, s, c), lambda i: (i, 0, 0)),
        out_shape=jax.ShapeDtypeStruct((b, s, c), F32),
        scratch_shapes=[pltpu.VMEM((V7X_SUBLANES, c), F32)],
        compiler_params=_params("parallel"),
        name="cumsum_positions",
    )(x, tri)


def _matmul_residual_kernel(a_ref, w_ref, r_ref, o_ref):
    o_ref[...] = r_ref[...] + jnp.dot(a_ref[...], w_ref[...], preferred_element_type=F32)


def matmul_residual(a, w, res, *, tm_target, tn_target):
    m, k = a.shape
    n = w.shape[1]
    tm = _pick_tile(m, tm_target, 2 * V7X_SUBLANES)
    tn = _pick_tile(n, tn_target, V7X_LANES)
    return pl.pallas_call(
        _matmul_residual_kernel,
        grid=(m // tm, n // tn),
        in_specs=[
            pl.BlockSpec((tm, k), lambda i, j: (i, 0)),
            pl.BlockSpec((k, tn), lambda i, j: (0, j)),
            pl.BlockSpec((tm, tn), lambda i, j: (i, j)),
        ],
        out_specs=pl.BlockSpec((tm, tn), lambda i, j: (i, j)),
        out_shape=jax.ShapeDtypeStruct((m, n), F32),
        compiler_params=_params("parallel", "parallel"),
        name="matmul_residual",
    )(a, w, res)


def _ffn_up_kernel(h_ref, g_ref, wg_ref, wv_ref, wc_ref, bc_ref, o_ref,
                   xn_ref, gbuf_ref, tail_ref, *, tiles_per_seq):
    i = pl.program_id(0)
    j = pl.program_id(1)
    tm, tn = o_ref.shape
    halo = V7X_SUBLANES

    @pl.when(j == 0)
    def _():
        _rms_to_scratch(h_ref, g_ref, xn_ref)

    xn = xn_ref[...]
    gate = jnp.dot(xn, wg_ref[...], preferred_element_type=F32)
    val = jnp.dot(xn, wv_ref[...], preferred_element_type=F32)

    seq_start = (i % tiles_per_seq) == 0

    @pl.when(seq_start)
    def _():
        gbuf_ref[0:halo, :] = jnp.zeros((halo, tn), F32)

    @pl.when(jnp.logical_not(seq_start))
    def _():
        gbuf_ref[0:halo, :] = tail_ref[j]

    gbuf_ref[halo:halo + tm, :] = gate
    tail_ref[j] = gate[tm - halo:tm, :]

    g1 = gbuf_ref[halo - 1:halo - 1 + tm, :]
    g2 = gbuf_ref[halo - 2:halo - 2 + tm, :]
    wc = wc_ref[...]
    conv = bc_ref[...] + (wc[0:1, :] * g2 + wc[1:2, :] * g1 + wc[2:3, :] * gate)
    act = conv * (1.0 / (1.0 + jnp.exp(-conv))) * val
    o_ref[...] = act.astype(o_ref.dtype)


def ffn_up(h, gain, w_up, w_conv, b_conv, seq, *, tm_target=1024, tn_target=256):
    m, d = h.shape
    f = w_up.shape[1] // 2
    tm = _pick_tile(seq, tm_target, NORM_ROW_CHUNK)
    tn = _pick_tile(f, tn_target, V7X_LANES)
    nj = f // tn
    halo = V7X_SUBLANES
    assert tm >= halo and CONV_WIDTH - 1 <= halo
    return pl.pallas_call(
        functools.partial(_ffn_up_kernel, tiles_per_seq=seq // tm),
        grid=(m // tm, nj),
        in_specs=[
            pl.BlockSpec((tm, d), lambda i, j: (i, 0)),
            pl.BlockSpec((1, d), lambda i, j: (0, 0)),
            pl.BlockSpec((d, tn), lambda i, j: (0, j)),
            pl.BlockSpec((d, tn), lambda i, j: (0, j + nj)),
            pl.BlockSpec((CONV_WIDTH, tn), lambda i, j: (0, j)),
            pl.BlockSpec((1, tn), lambda i, j: (0, j)),
        ],
        out_specs=pl.BlockSpec((tm, tn), lambda i, j: (i, j)),
        out_shape=jax.ShapeDtypeStruct((m, f), BF16),
        scratch_shapes=[
            pltpu.VMEM((tm, d), BF16),
            pltpu.VMEM((tm + halo, tn), F32),
            pltpu.VMEM((nj, halo, tn), F32),
        ],
        compiler_params=_params("arbitrary", "arbitrary"),
        name="ffn_up",
    )(h, gain.reshape(1, d), w_up, w_up, w_conv, b_conv.reshape(1, f))


def _sb_attn_kernel(q_ref, k_ref, v_ref, u2_ref, o_ref,
                    acc_ref, run_ref, z_ref, hl_ref, incl_ref, w_ref, *, tq, tk, heads, scale):
    seq = q_ref.shape[0]
    nsub = tq // tk
    reps = tk // V7X_LANES
    score_scale = scale * LOG2_E
    nchunk = tq // ATTN_ROW_CHUNK

    def strict_mask(r):
        shape = (ATTN_ROW_CHUNK, tq)
        return (lax.broadcasted_iota(jnp.int32, shape, 1)
                < r * ATTN_ROW_CHUNK + lax.broadcasted_iota(jnp.int32, shape, 0))

    def one_block(q0, kb, masked):
        k0 = pl.multiple_of(kb * tq, tq)
        for hh in range(heads):
            cols_h = slice(hh * HEAD_DIM, (hh + 1) * HEAD_DIM)
            z_ref[hh] = lax.dot_general(q_ref[pl.ds(q0, tq), cols_h], k_ref[pl.ds(k0, tq), cols_h],
                                        (((1,), (1,)), ((), ())), preferred_element_type=F32)
            for r in range(nchunk):
                rows = slice(r * ATTN_ROW_CHUNK, (r + 1) * ATTN_ROW_CHUNK)
                z2 = z_ref[hh, rows, :] * score_scale
                nz2 = -z2
                lsm = jnp.minimum(nz2, 0.0) - jnp.log2(1.0 + jnp.exp2(jnp.minimum(z2, nz2)))
                if masked:
                    lsm = jnp.where(strict_mask(r), lsm, 0.0)
                for c in range(nsub):
                    sub = lsm[:, c * tk:(c + 1) * tk]
                    hi = sub.astype(BF16)
                    hl_ref[hh, c, rows, 0:tk] = hi
                    hl_ref[hh, c, rows, tk:2 * tk] = (sub - hi.astype(F32)).astype(BF16)
            for c in range(nsub):
                incl_ref[hh, c] = jnp.dot(hl_ref[hh, c], u2_ref[...], preferred_element_type=F32)
            for r in range(nchunk):
                rows = slice(r * ATTN_ROW_CHUNK, (r + 1) * ATTN_ROW_CHUNK)
                run = run_ref[hh, rows, :]
                for c in reversed(range(nsub)):
                    cols = slice(c * tk, (c + 1) * tk)
                    incl = incl_ref[hh, c, rows, :]
                    e2 = z_ref[hh, rows, cols] * score_scale + incl + jnp.tile(run, (1, reps))
                    w = jnp.exp2(e2)
                    if masked:
                        w = jnp.where(strict_mask(r)[:, cols], w, 0.0)
                    w_ref[hh, rows, cols] = w.astype(BF16)
                    run = run + jnp.broadcast_to(incl[:, 0:1], run.shape)
                run_ref[hh, rows, :] = run
            acc_ref[hh] += jnp.dot(w_ref[hh], v_ref[pl.ds(k0, tq), cols_h],
                                   preferred_element_type=F32)

    def q_body(qi, carry):
        q0 = pl.multiple_of(qi * tq, tq)
        acc_ref[...] = jnp.zeros_like(acc_ref)
        run_ref[...] = jnp.zeros_like(run_ref)
        one_block(q0, qi, True)

        def kb_body(n, c):
            one_block(q0, qi - 1 - n, False)
            return c

        lax.fori_loop(0, qi, kb_body, 0)
        for hh in range(heads):
            o_ref[pl.ds(q0, tq), hh * HEAD_DIM:(hh + 1) * HEAD_DIM] = acc_ref[hh].astype(o_ref.dtype)
        return carry

    lax.fori_loop(0, seq // tq, q_body, 0)


def stick_breaking_attention(qkv, n_heads, *, tq_target=512, tk=V7X_MXU_DIM, heads_per_step=2):
    b, s, _ = qkv.shape
    tk = min(tk, s)
    tq = _pick_tile(s, tq_target, tk)
    hps = heads_per_step if n_heads % heads_per_step == 0 else 1
    nstep = n_heads // hps
    width = hps * HEAD_DIM
    u = (lax.broadcasted_iota(jnp.int32, (tk, tk), 0)
         >= lax.broadcasted_iota(jnp.int32, (tk, tk), 1)).astype(BF16)
    u2 = jnp.concatenate([u, u], axis=0)
    kern = functools.partial(_sb_attn_kernel, tq=tq, tk=tk, heads=hps, scale=HEAD_DIM ** -0.5)
    return pl.pallas_call(
        kern,
        grid=(b, nstep),
        in_specs=[
            pl.BlockSpec((None, s, width), lambda bi, h: (bi, 0, h)),
            pl.BlockSpec((None, s, width), lambda bi, h: (bi, 0, nstep + h)),
            pl.BlockSpec((None, s, width), lambda bi, h: (bi, 0, 2 * nstep + h)),
            pl.BlockSpec((2 * tk, tk), lambda bi, h: (0, 0)),
        ],
        out_specs=pl.BlockSpec((None, s, width), lambda bi, h: (bi, 0, h)),
        out_shape=jax.ShapeDtypeStruct((b, s, n_heads * HEAD_DIM), BF16),
        scratch_shapes=[
            pltpu.VMEM((hps, tq, HEAD_DIM), F32),
            pltpu.VMEM((hps, tq, V7X_LANES), F32),
            pltpu.VMEM((hps, tq, tq), F32),
            pltpu.VMEM((hps, tq // tk, tq, 2 * tk), BF16),
            pltpu.VMEM((hps, tq // tk, tq, tk), F32),
            pltpu.VMEM((hps, tq, tq), BF16),
        ],
        compiler_params=_params("parallel", "parallel"),
        name="stick_breaking_attention",
    )(qkv, qkv, qkv, u2)


def _fox_attn_kernel(q_ref, k_ref, v_ref, ccol_ref, crow_ref, o_ref,
                     acc_ref, m_ref, l_ref, bt_ref, s_ref, p_ref, *, tq, group, scale):
    seq = k_ref.shape[0]
    kvh = pl.program_id(1)
    reps = tq // V7X_LANES
    score_scale = scale * LOG2_E

    def one_block(q0, kb, masked):
        k0 = pl.multiple_of(kb * tq, tq)
        kblk = k_ref[pl.ds(k0, tq), :]
        vblk = v_ref[pl.ds(k0, tq), :]
        for g in range(group):
            heads = slice(g * HEAD_DIM, (g + 1) * HEAD_DIM)
            s_ref[g] = lax.dot_general(q_ref[pl.ds(q0, tq), heads], kblk,
                                       (((1,), (1,)), ((), ())), preferred_element_type=F32)
            cs = crow_ref[g:g + 1, pl.ds(k0, tq)] * LOG2_E
            for r in range(tq // ATTN_ROW_CHUNK):
                rows = slice(r * ATTN_ROW_CHUNK, (r + 1) * ATTN_ROW_CHUNK)
                grows = slice(g * tq + rows.start, g * tq + rows.stop)
                u = s_ref[g, rows, :] * score_scale - cs
                if masked:
                    shape = (ATTN_ROW_CHUNK, tq)
                    causal = (lax.broadcasted_iota(jnp.int32, shape, 1)
                              <= rows.start + lax.broadcasted_iota(jnp.int32, shape, 0))
                    u = jnp.where(causal, u, MASK_VALUE)
                bt = bt_ref[grows, :]
                m_old = m_ref[grows, :]
                m_new = jnp.maximum(m_old, jnp.max(u, axis=-1, keepdims=True) + bt)
                alpha = jnp.exp2(m_old - m_new)
                p = jnp.exp2(u - jnp.tile(m_new - bt, (1, reps)))
                l_ref[grows, :] = alpha * l_ref[grows, :] + jnp.sum(p, axis=-1, keepdims=True)
                m_ref[grows, :] = m_new
                acc_ref[grows, :] = alpha * acc_ref[grows, :]
                p_ref[g, rows, :] = p.astype(BF16)
            acc_ref[g * tq:(g + 1) * tq, :] += jnp.dot(p_ref[g], vblk, preferred_element_type=F32)

    def q_body(qi, carry):
        q0 = pl.multiple_of(qi * tq, tq)
        cblk = ccol_ref[pl.ds(q0, tq), :]
        lane = lax.broadcasted_iota(jnp.int32, cblk.shape, 1)
        for g in range(group):
            c_t = jnp.sum(jnp.where(lane == kvh * group + g, cblk, 0.0), axis=-1, keepdims=True)
            bt_ref[g * tq:(g + 1) * tq, :] = jnp.broadcast_to(c_t * LOG2_E, (tq, V7X_LANES))
        acc_ref[...] = jnp.zeros_like(acc_ref)
        l_ref[...] = jnp.zeros_like(l_ref)
        m_ref[...] = jnp.full_like(m_ref, MASK_VALUE)
        one_block(q0, qi, True)

        def kb_body(kb, c):
            one_block(q0, kb, False)
            return c

        lax.fori_loop(0, qi, kb_body, 0)
        out = acc_ref[...] / l_ref[...]
        for g in range(group):
            o_ref[pl.ds(q0, tq), g * HEAD_DIM:(g + 1) * HEAD_DIM] = (
                out[g * tq:(g + 1) * tq, :].astype(o_ref.dtype))
        return carry

    lax.fori_loop(0, seq // tq, q_body, 0)


def forgetting_attention(q, kv, c_col, c_row, n_kv_heads, *, tq_target=512):
    b, s, dq = q.shape
    assert HEAD_DIM == V7X_LANES
    group = dq // HEAD_DIM // n_kv_heads
    tq = _pick_tile(s, tq_target, V7X_LANES)
    cw = c_col.shape[2]
    kern = functools.partial(_fox_attn_kernel, tq=tq, group=group, scale=HEAD_DIM ** -0.5)
    stat = pltpu.VMEM((group * tq, V7X_LANES), F32)
    return pl.pallas_call(
        kern,
        grid=(b, n_kv_heads),
        in_specs=[
            pl.BlockSpec((None, s, group * HEAD_DIM), lambda bi, h: (bi, 0, h)),
            pl.BlockSpec((None, s, HEAD_DIM), lambda bi, h: (bi, 0, h)),
            pl.BlockSpec((None, s, HEAD_DIM), lambda bi, h: (bi, 0, n_kv_heads + h)),
            pl.BlockSpec((None, s, cw), lambda bi, h: (bi, 0, 0)),
            pl.BlockSpec((None, None, group, s), lambda bi, h: (bi, h, 0, 0)),
        ],
        out_specs=pl.BlockSpec((None, s, group * HEAD_DIM), lambda bi, h: (bi, 0, h)),
        out_shape=jax.ShapeDtypeStruct((b, s, dq), BF16),
        scratch_shapes=[stat, stat, stat, stat,
                        pltpu.VMEM((group, tq, tq), F32),
                        pltpu.VMEM((group, tq, tq), BF16)],
        compiler_params=_params("parallel", "parallel"),
        name="forgetting_attention",
    )(q, kv, kv, c_col, c_row)


def _rms_norm_kernel(h_ref, g_ref, o_ref):
    x = h_ref[...]
    ms = jnp.mean(x * x, axis=-1, keepdims=True)
    o_ref[...] = x * lax.rsqrt(ms + RMS_EPS) * g_ref[...]


def rms_norm(h, gain, *, tm_target=256):
    m, d = h.shape
    tm = _pick_tile(m, tm_target, V7X_SUBLANES)
    return pl.pallas_call(
        _rms_norm_kernel,
        grid=(m // tm,),
        in_specs=[pl.BlockSpec((tm, d), lambda i: (i, 0)),
                  pl.BlockSpec((1, d), lambda i: (0, 0))],
        out_specs=pl.BlockSpec((tm, d), lambda i: (i, 0)),
        out_shape=jax.ShapeDtypeStruct((m, d), F32),
        compiler_params=_params("parallel"),
        name="final_rms_norm",
    )(h, gain.reshape(1, d))


def _conv_glu_ffn(h, gain, w_up, w_conv, b_conv, w_down, seq):
    act = ffn_up(h, gain, w_up.astype(BF16), w_conv, b_conv, seq)
    return matmul_residual(act, w_down.astype(BF16), h, tm_target=512, tn_target=256)


def kernel(x, a_attn_norm, a_w_qkv, a_w_o, a_ffn_norm, a_w_up, a_w_conv, a_b_conv, a_w_down,
           kv_norm, w_kv, w_f, b_f,
           b_attn_norm, b_w_q, b_w_o, b_ffn_norm, b_w_up, b_w_conv, b_b_conv, b_w_down,
           final_norm):
    bsz, seq, d = x.shape
    m = bsz * seq
    n_heads = d // HEAD_DIM
    n_kv_heads = w_kv.shape[1] // (2 * HEAD_DIM)
    group = n_heads // n_kv_heads
    n_gates = w_f.shape[1]
    assert n_gates == n_heads and n_gates <= V7X_LANES

    h = x.reshape(m, d)

    for i in range(a_w_qkv.shape[0]):
        qkv = norm_matmul(h, a_attn_norm[i], a_w_qkv[i].astype(BF16))
        o = stick_breaking_attention(qkv.reshape(bsz, seq, 3 * d), n_heads)
        h = matmul_residual(o.reshape(m, d), a_w_o[i].astype(BF16), h, tm_target=1024, tn_target=512)
        h = _conv_glu_ffn(h, a_ffn_norm[i], a_w_up[i], a_w_conv[i], a_b_conv[i], a_w_down[i], seq)

    kv = c_col = c_row = None
    for j in range(b_w_q.shape[0]):
        if j == 0:
            wf_pad = jnp.pad(w_f, ((0, 0), (0, V7X_LANES - n_gates))).astype(BF16)
            bf_pad = jnp.pad(b_f, (0, V7X_LANES - n_gates)).reshape(1, V7X_LANES)
            kv, log_f = kv_proj(h, kv_norm, w_kv.astype(BF16), wf_pad, bf_pad)
            kv = kv.reshape(bsz, seq, 2 * n_kv_heads * HEAD_DIM)
            c_col = cumsum_positions(log_f.reshape(bsz, seq, V7X_LANES))
            c_row = (c_col[:, :, :n_gates].transpose(0, 2, 1)
                     .reshape(bsz, n_kv_heads, group, seq))
        q = norm_matmul(h, b_attn_norm[j], b_w_q[j].astype(BF16))
        o = forgetting_attention(q.reshape(bsz, seq, d), kv, c_col, c_row, n_kv_heads)
        h = matmul_residual(o.reshape(m, d), b_w_o[j].astype(BF16), h, tm_target=1024, tn_target=512)
        h = _conv_glu_ffn(h, b_ffn_norm[j], b_w_up[j], b_w_conv[j], b_b_conv[j], b_w_down[j], seq)

    return rms_norm(h, final_norm).reshape(bsz, seq, d)
```

```python
import functools

import jax
import jax.numpy as jnp
from jax import lax
from jax.experimental import pallas as pl
from jax.experimental.pallas import tpu as pltpu

HEAD_DIM = 128
CONV_WIDTH = 3
RMS_EPS = 1e-6

V7X_LANES = 128
V7X_SUBLANES = 8
V7X_MXU_DIM = 256
V7X_VMEM_LIMIT_BYTES = 56 * 1024 * 1024

F32 = jnp.float32
BF16 = jnp.bfloat16

NORM_ROW_CHUNK = 32
ATTN_ROW_CHUNK = 32
MASK_VALUE = -1e30
LOG2_E = 1.4426950408889634
F32_SIGN_BIT = 0x80000000


def _params(*semantics):
    return pltpu.CompilerParams(dimension_semantics=semantics,
                                vmem_limit_bytes=V7X_VMEM_LIMIT_BYTES)


def _pick_tile(n, target, quantum):
    if n <= target:
        return n
    t = (target // quantum) * quantum
    while t >= quantum:
        if n % t == 0:
            return t
        t -= quantum
    raise ValueError(f"no tile for {n}")


def _rms_to_scratch(h_ref, g_ref, xn_ref):
    tm = h_ref.shape[0]

    def body(c, carry):
        r0 = pl.multiple_of(c * NORM_ROW_CHUNK, NORM_ROW_CHUNK)
        x = h_ref[pl.ds(r0, NORM_ROW_CHUNK), :]
        ms = jnp.mean(x * x, axis=-1, keepdims=True)
        y = x * lax.rsqrt(ms + RMS_EPS) * g_ref[...]
        xn_ref[pl.ds(r0, NORM_ROW_CHUNK), :] = y.astype(xn_ref.dtype)
        return carry

    lax.fori_loop(0, tm // NORM_ROW_CHUNK, body, 0, unroll=2)


def _norm_matmul_kernel(h_ref, g_ref, w_ref, o_ref, xn_ref):
    @pl.when(pl.program_id(1) == 0)
    def _():
        _rms_to_scratch(h_ref, g_ref, xn_ref)

    o_ref[...] = jnp.dot(xn_ref[...], w_ref[...],
                         preferred_element_type=F32).astype(o_ref.dtype)


def norm_matmul(h, gain, w, *, tm_target=1024, tn_target=512):
    m, d = h.shape
    n = w.shape[1]
    tm = _pick_tile(m, tm_target, NORM_ROW_CHUNK)
    tn = _pick_tile(n, tn_target, V7X_LANES)
    return pl.pallas_call(
        _norm_matmul_kernel,
        grid=(m // tm, n // tn),
        in_specs=[
            pl.BlockSpec((tm, d), lambda i, j: (i, 0)),
            pl.BlockSpec((1, d), lambda i, j: (0, 0)),
            pl.BlockSpec((d, tn), lambda i, j: (0, j)),
        ],
        out_specs=pl.BlockSpec((tm, tn), lambda i, j: (i, j)),
        out_shape=jax.ShapeDtypeStruct((m, n), BF16),
        scratch_shapes=[pltpu.VMEM((tm, d), BF16)],
        compiler_params=_params("parallel", "arbitrary"),
        name="norm_matmul",
    )(h, gain.reshape(1, d), w)


def _kv_proj_kernel(h_ref, g_ref, w_ref, wf_ref, bf_ref, o_ref, lf_ref, xn_ref):
    @pl.when(pl.program_id(1) == 0)
    def _():
        _rms_to_scratch(h_ref, g_ref, xn_ref)
        pre = jnp.dot(xn_ref[...], wf_ref[...], preferred_element_type=F32) + bf_ref[...]
        lf_ref[...] = -(jnp.maximum(-pre, 0.0) + jnp.log(1.0 + jnp.exp(-jnp.abs(pre))))

    o_ref[...] = jnp.dot(xn_ref[...], w_ref[...],
                         preferred_element_type=F32).astype(o_ref.dtype)


def kv_proj(h, gain, w_kv, wf_pad, bf_pad, *, tm_target=1024, tn_target=512):
    m, d = h.shape
    n = w_kv.shape[1]
    nf = wf_pad.shape[1]
    tm = _pick_tile(m, tm_target, NORM_ROW_CHUNK)
    tn = _pick_tile(n, tn_target, V7X_LANES)
    return pl.pallas_call(
        _kv_proj_kernel,
        grid=(m // tm, n // tn),
        in_specs=[
            pl.BlockSpec((tm, d), lambda i, j: (i, 0)),
            pl.BlockSpec((1, d), lambda i, j: (0, 0)),
            pl.BlockSpec((d, tn), lambda i, j: (0, j)),
            pl.BlockSpec((d, nf), lambda i, j: (0, 0)),
            pl.BlockSpec((1, nf), lambda i, j: (0, 0)),
        ],
        out_specs=[
            pl.BlockSpec((tm, tn), lambda i, j: (i, j)),
            pl.BlockSpec((tm, nf), lambda i, j: (i, 0)),
        ],
        out_shape=[
            jax.ShapeDtypeStruct((m, n), BF16),
            jax.ShapeDtypeStruct((m, nf), F32),
        ],
        scratch_shapes=[pltpu.VMEM((tm, d), BF16)],
        compiler_params=_params("parallel", "arbitrary"),
        name="kv_proj",
    )(h, gain.reshape(1, d), w_kv, wf_pad, bf_pad)


def _split3(x):
    hi = x.astype(BF16)
    r1 = x - hi.astype(F32)
    mid = r1.astype(BF16)
    lo = (r1 - mid.astype(F32)).astype(BF16)
    return hi, mid, lo


def _cumsum_kernel(x_ref, tri_ref, c_ref, run_ref):
    blk = tri_ref.shape[0]
    nblk = x_ref.shape[0] // blk
    run_ref[...] = jnp.zeros_like(run_ref)

    def body(n, carry):
        r0 = pl.multiple_of(n * blk, blk)
        hi, mid, lo = _split3(x_ref[pl.ds(r0, blk), :])
        tri = tri_ref[...]
        incl = (jnp.dot(tri, hi, preferred_element_type=F32)
                + jnp.dot(tri, mid, preferred_element_type=F32)
                + jnp.dot(tri, lo, preferred_element_type=F32)
                + run_ref[0:1, :])
        c_ref[pl.ds(r0, blk), :] = incl
        run_ref[0:1, :] = incl[blk - 1:blk, :]
        return carry

    lax.fori_loop(0, nblk, body, 0)


def cumsum_positions(x):
    b, s, c = x.shape
    blk = V7X_LANES
    tri = (lax.broadcasted_iota(jnp.int32, (blk, blk), 1)
           <= lax.broadcasted_iota(jnp.int32, (blk, blk), 0)).astype(BF16)
    return pl.pallas_call(
        _cumsum_kernel,
        grid=(b,),
        in_specs=[
            pl.BlockSpec((None, s, c), lambda i: (i, 0, 0)),
            pl.BlockSpec((blk, blk), lambda i: (0, 0)),
        ],
        out_specs=pl.BlockSpec((None, s, c), lambda i: (i, 0, 0)),
        out_shape=jax.ShapeDtypeStruct((b, s, c), F32),
        scratch_shapes=[pltpu.VMEM((V7X_SUBLANES, c), F32)],
        compiler_params=_params("parallel"),
        name="cumsum_positions",
    )(x, tri)


def _matmul_residual_kernel(a_ref, w_ref, r_ref, o_ref):
    o_ref[...] = r_ref[...] + jnp.dot(a_ref[...], w_ref[...], preferred_element_type=F32)


def matmul_residual(a, w, res, *, tm_target, tn_target):
    m, k = a.shape
    n = w.shape[1]
    tm = _pick_tile(m, tm_target, 2 * V7X_SUBLANES)
    tn = _pick_tile(n, tn_target, V7X_LANES)
    return pl.pallas_call(
        _matmul_residual_kernel,
        grid=(m // tm, n // tn),
        in_specs=[
            pl.BlockSpec((tm, k), lambda i, j: (i, 0)),
            pl.BlockSpec((k, tn), lambda i, j: (0, j)),
            pl.BlockSpec((tm, tn), lambda i, j: (i, j)),
        ],
        out_specs=pl.BlockSpec((tm, tn), lambda i, j: (i, j)),
        out_shape=jax.ShapeDtypeStruct((m, n), F32),
        compiler_params=_params("parallel", "parallel"),
        name="matmul_residual",
    )(a, w, res)


def _ffn_up_kernel(h_ref, g_ref, wg_ref, wv_ref, wc_ref, bc_ref, o_ref,
                   xn_ref, halo_ref, tail_ref, *, tiles_per_seq, row_splits):
    i = pl.program_id(0)
    j = pl.program_id(1)
    tm, tn = o_ref.shape
    halo = V7X_SUBLANES
    mc = tm // row_splits

    @pl.when(j == 0)
    def _():
        _rms_to_scratch(h_ref, g_ref, xn_ref)

    seq_start = (i % tiles_per_seq) == 0

    @pl.when(seq_start)
    def _():
        halo_ref[...] = jnp.zeros_like(halo_ref)

    @pl.when(jnp.logical_not(seq_start))
    def _():
        halo_ref[...] = tail_ref[j]

    wc = wc_ref[...]
    prev_rows = halo_ref[...]
    for c in range(row_splits):
        rows = slice(c * mc, (c + 1) * mc)
        xc = xn_ref[rows, :]
        gate = jnp.dot(xc, wg_ref[...], preferred_element_type=F32)
        val = jnp.dot(xc, wv_ref[...], preferred_element_type=F32)
        ext = jnp.concatenate([prev_rows, gate], axis=0)
        g1 = pltpu.roll(ext, 1, 0)[halo:, :]
        g2 = pltpu.roll(ext, 2, 0)[halo:, :]
        conv = bc_ref[...] + (wc[0:1, :] * g2 + wc[1:2, :] * g1 + wc[2:3, :] * gate)
        act = conv * (1.0 / (1.0 + jnp.exp(-conv))) * val
        o_ref[rows, :] = act.astype(o_ref.dtype)
        prev_rows = gate[mc - halo:mc, :]
    tail_ref[j] = prev_rows


def ffn_up(h, gain, w_up, w_conv, b_conv, seq, *, tm_target=1024, tn_target=256):
    m, d = h.shape
    f = w_up.shape[1] // 2
    tm = _pick_tile(seq, tm_target, NORM_ROW_CHUNK)
    tn = _pick_tile(f, tn_target, V7X_LANES)
    nj = f // tn
    halo = V7X_SUBLANES
    row_splits = 4 if tm % (4 * NORM_ROW_CHUNK) == 0 else 1
    assert tm // row_splits >= halo and CONV_WIDTH - 1 <= halo
    return pl.pallas_call(
        functools.partial(_ffn_up_kernel, tiles_per_seq=seq // tm, row_splits=row_splits),
        grid=(m // tm, nj),
        in_specs=[
            pl.BlockSpec((tm, d), lambda i, j: (i, 0)),
            pl.BlockSpec((1, d), lambda i, j: (0, 0)),
            pl.BlockSpec((d, tn), lambda i, j: (0, j)),
            pl.BlockSpec((d, tn), lambda i, j: (0, j + nj)),
            pl.BlockSpec((CONV_WIDTH, tn), lambda i, j: (0, j)),
            pl.BlockSpec((1, tn), lambda i, j: (0, j)),
        ],
        out_specs=pl.BlockSpec((tm, tn), lambda i, j: (i, j)),
        out_shape=jax.ShapeDtypeStruct((m, f), BF16),
        scratch_shapes=[
            pltpu.VMEM((tm, d), BF16),
            pltpu.VMEM((halo, tn), F32),
            pltpu.VMEM((nj, halo, tn), F32),
        ],
        compiler_params=_params("arbitrary", "arbitrary"),
        name="ffn_up",
    )(h, gain.reshape(1, d), w_up, w_up, w_conv, b_conv.reshape(1, f))


def _sb_attn_kernel(q_ref, k_ref, v_ref, u2_ref, o_ref,
                    acc_ref, run_ref, z_ref, hl_ref, incl_ref, w_ref, *, tq, tk, heads, scale):
    seq = q_ref.shape[0]
    nsub = tq // tk
    reps = tk // V7X_LANES
    score_scale = scale * LOG2_E
    nchunk = tq // ATTN_ROW_CHUNK

    def strict_mask(r):
        shape = (ATTN_ROW_CHUNK, tq)
        return (lax.broadcasted_iota(jnp.int32, shape, 1)
                < r * ATTN_ROW_CHUNK + lax.broadcasted_iota(jnp.int32, shape, 0))

    def one_block(q0, kb, masked):
        k0 = pl.multiple_of(kb * tq, tq)
        for hh in range(heads):
            cols_h = slice(hh * HEAD_DIM, (hh + 1) * HEAD_DIM)
            z_ref[hh] = lax.dot_general(q_ref[pl.ds(q0, tq), cols_h], k_ref[pl.ds(k0, tq), cols_h],
                                        (((1,), (1,)), ((), ())), preferred_element_type=F32)
            for r in range(nchunk):
                rows = slice(r * ATTN_ROW_CHUNK, (r + 1) * ATTN_ROW_CHUNK)
                nz2 = z_ref[hh, rows, :] * (-score_scale)
                z_ref[hh, rows, :] = nz2
                neg_abs = lax.bitcast_convert_type(
                    lax.bitcast_convert_type(nz2, jnp.uint32) | jnp.uint32(F32_SIGN_BIT), F32)
                lsm = jnp.minimum(nz2, 0.0) - jnp.log2(1.0 + jnp.exp2(neg_abs))
                if masked:
                    lsm = jnp.where(strict_mask(r), lsm, 0.0)
                for c in range(nsub):
                    sub = lsm[:, c * tk:(c + 1) * tk]
                    hi = sub.astype(BF16)
                    hl_ref[hh, c, rows, 0:tk] = hi
                    hl_ref[hh, c, rows, tk:2 * tk] = (sub - hi.astype(F32)).astype(BF16)
            for c in range(nsub):
                incl_ref[hh, c] = jnp.dot(hl_ref[hh, c], u2_ref[...], preferred_element_type=F32)
            for r in range(nchunk):
                rows = slice(r * ATTN_ROW_CHUNK, (r + 1) * ATTN_ROW_CHUNK)
                run = run_ref[hh, rows, :]
                for c in reversed(range(nsub)):
                    cols = slice(c * tk, (c + 1) * tk)
                    incl = incl_ref[hh, c, rows, :]
                    e2 = incl + jnp.tile(run, (1, reps)) - z_ref[hh, rows, cols]
                    w = jnp.exp2(e2)
                    if masked:
                        w = jnp.where(strict_mask(r)[:, cols], w, 0.0)
                    w_ref[hh, rows, cols] = w.astype(BF16)
                    run = run + jnp.broadcast_to(incl[:, 0:1], run.shape)
                run_ref[hh, rows, :] = run
            acc_ref[hh] += jnp.dot(w_ref[hh], v_ref[pl.ds(k0, tq), cols_h],
                                   preferred_element_type=F32)

    def q_body(qi, carry):
        q0 = pl.multiple_of(qi * tq, tq)
        acc_ref[...] = jnp.zeros_like(acc_ref)
        run_ref[...] = jnp.zeros_like(run_ref)
        one_block(q0, qi, True)

        def kb_body(n, c):
            one_block(q0, qi - 1 - n, False)
            return c

        lax.fori_loop(0, qi, kb_body, 0)
        for hh in range(heads):
            o_ref[pl.ds(q0, tq), hh * HEAD_DIM:(hh + 1) * HEAD_DIM] = acc_ref[hh].astype(o_ref.dtype)
        return carry

    lax.fori_loop(0, seq // tq, q_body, 0)


def stick_breaking_attention(qkv, n_heads, *, tq_target=512, tk=V7X_MXU_DIM, heads_per_step=4):
    b, s, _ = qkv.shape
    tk = min(tk, s)
    tq = _pick_tile(s, tq_target, tk)
    hps = heads_per_step if n_heads % heads_per_step == 0 else 1
    nstep = n_heads // hps
    width = hps * HEAD_DIM
    u = (lax.broadcasted_iota(jnp.int32, (tk, tk), 0)
         >= lax.broadcasted_iota(jnp.int32, (tk, tk), 1)).astype(BF16)
    u2 = jnp.concatenate([u, u], axis=0)
    kern = functools.partial(_sb_attn_kernel, tq=tq, tk=tk, heads=hps, scale=HEAD_DIM ** -0.5)
    return pl.pallas_call(
        kern,
        grid=(b, nstep),
        in_specs=[
            pl.BlockSpec((None, s, width), lambda bi, h: (bi, 0, h)),
            pl.BlockSpec((None, s, width), lambda bi, h: (bi, 0, nstep + h)),
            pl.BlockSpec((None, s, width), lambda bi, h: (bi, 0, 2 * nstep + h)),
            pl.BlockSpec((2 * tk, tk), lambda bi, h: (0, 0)),
        ],
        out_specs=pl.BlockSpec((None, s, width), lambda bi, h: (bi, 0, h)),
        out_shape=jax.ShapeDtypeStruct((b, s, n_heads * HEAD_DIM), BF16),
        scratch_shapes=[
            pltpu.VMEM((hps, tq, HEAD_DIM), F32),
            pltpu.VMEM((hps, tq, V7X_LANES), F32),
            pltpu.VMEM((hps, tq, tq), F32),
            pltpu.VMEM((hps, tq // tk, tq, 2 * tk), BF16),
            pltpu.VMEM((hps, tq // tk, tq, tk), F32),
            pltpu.VMEM((hps, tq, tq), BF16),
        ],
        compiler_params=_params("parallel", "parallel"),
        name="stick_breaking_attention",
    )(qkv, qkv, qkv, u2)


def _fox_attn_kernel(q_ref, k_ref, v_ref, ccol_ref, crow_ref, o_ref,
                     acc_ref, m_ref, l_ref, bt_ref, s_ref, p_ref, *, tq, group, scale):
    seq = k_ref.shape[0]
    kvh = pl.program_id(1)
    reps = tq // V7X_LANES
    score_scale = scale * LOG2_E

    def one_block(q0, kb, masked):
        k0 = pl.multiple_of(kb * tq, tq)
        kblk = k_ref[pl.ds(k0, tq), :]
        vblk = v_ref[pl.ds(k0, tq), :]
        for g in range(group):
            heads = slice(g * HEAD_DIM, (g + 1) * HEAD_DIM)
            s_ref[g] = lax.dot_general(q_ref[pl.ds(q0, tq), heads], kblk,
                                       (((1,), (1,)), ((), ())), preferred_element_type=F32)
            cs = crow_ref[g:g + 1, pl.ds(k0, tq)] * LOG2_E
            for r in range(tq // ATTN_ROW_CHUNK):
                rows = slice(r * ATTN_ROW_CHUNK, (r + 1) * ATTN_ROW_CHUNK)
                grows = slice(g * tq + rows.start, g * tq + rows.stop)
                u = s_ref[g, rows, :] * score_scale - cs
                if masked:
                    shape = (ATTN_ROW_CHUNK, tq)
                    causal = (lax.broadcasted_iota(jnp.int32, shape, 1)
                              <= rows.start + lax.broadcasted_iota(jnp.int32, shape, 0))
                    u = jnp.where(causal, u, MASK_VALUE)
                bt = bt_ref[grows, :]
                m_old = m_ref[grows, :]
                m_new = jnp.maximum(m_old, jnp.max(u, axis=-1, keepdims=True) + bt)
                alpha = jnp.exp2(m_old - m_new)
                p = jnp.exp2(u - jnp.tile(m_new - bt, (1, reps)))
                l_ref[grows, :] = alpha * l_ref[grows, :] + jnp.sum(p, axis=-1, keepdims=True)
                m_ref[grows, :] = m_new
                acc_ref[grows, :] = alpha * acc_ref[grows, :]
                p_ref[g, rows, :] = p.astype(BF16)
            acc_ref[g * tq:(g + 1) * tq, :] += jnp.dot(p_ref[g], vblk, preferred_element_type=F32)

    def q_body(qi, carry):
        q0 = pl.multiple_of(qi * tq, tq)
        cblk = ccol_ref[pl.ds(q0, tq), :]
        lane = lax.broadcasted_iota(jnp.int32, cblk.shape, 1)
        for g in range(group):
            c_t = jnp.sum(jnp.where(lane == kvh * group + g, cblk, 0.0), axis=-1, keepdims=True)
            bt_ref[g * tq:(g + 1) * tq, :] = jnp.broadcast_to(c_t * LOG2_E, (tq, V7X_LANES))
        acc_ref[...] = jnp.zeros_like(acc_ref)
        l_ref[...] = jnp.zeros_like(l_ref)
        m_ref[...] = jnp.full_like(m_ref, MASK_VALUE)
        one_block(q0, qi, True)

        def kb_body(kb, c):
            one_block(q0, kb, False)
            return c

        lax.fori_loop(0, qi, kb_body, 0)
        out = acc_ref[...] / l_ref[...]
        for g in range(group):
            o_ref[pl.ds(q0, tq), g * HEAD_DIM:(g + 1) * HEAD_DIM] = (
                out[g * tq:(g + 1) * tq, :].astype(o_ref.dtype))
        return carry

    lax.fori_loop(0, seq // tq, q_body, 0)


def forgetting_attention(q, kv, c_col, c_row, n_kv_heads, *, tq_target=512):
    b, s, dq = q.shape
    assert HEAD_DIM == V7X_LANES
    group = dq // HEAD_DIM // n_kv_heads
    tq = _pick_tile(s, tq_target, V7X_LANES)
    cw = c_col.shape[2]
    kern = functools.partial(_fox_attn_kernel, tq=tq, group=group, scale=HEAD_DIM ** -0.5)
    stat = pltpu.VMEM((group * tq, V7X_LANES), F32)
    return pl.pallas_call(
        kern,
        grid=(b, n_kv_heads),
        in_specs=[
            pl.BlockSpec((None, s, group * HEAD_DIM), lambda bi, h: (bi, 0, h)),
            pl.BlockSpec((None, s, HEAD_DIM), lambda bi, h: (bi, 0, h)),
            pl.BlockSpec((None, s, HEAD_DIM), lambda bi, h: (bi, 0, n_kv_heads + h)),
            pl.BlockSpec((None, s, cw), lambda bi, h: (bi, 0, 0)),
            pl.BlockSpec((None, None, group, s), lambda bi, h: (bi, h, 0, 0)),
        ],
        out_specs=pl.BlockSpec((None, s, group * HEAD_DIM), lambda bi, h: (bi, 0, h)),
        out_shape=jax.ShapeDtypeStruct((b, s, dq), BF16),
        scratch_shapes=[stat, stat, stat, stat,
                        pltpu.VMEM((group, tq, tq), F32),
                        pltpu.VMEM((group, tq, tq), BF16)],
        compiler_params=_params("parallel", "parallel"),
        name="forgetting_attention",
    )(q, kv, kv, c_col, c_row)


def _rms_norm_kernel(h_ref, g_ref, o_ref):
    x = h_ref[...]
    ms = jnp.mean(x * x, axis=-1, keepdims=True)
    o_ref[...] = x * lax.rsqrt(ms + RMS_EPS) * g_ref[...]


def rms_norm(h, gain, *, tm_target=256):
    m, d = h.shape
    tm = _pick_tile(m, tm_target, V7X_SUBLANES)
    return pl.pallas_call(
        _rms_norm_kernel,
        grid=(m // tm,),
        in_specs=[pl.BlockSpec((tm, d), lambda i: (i, 0)),
                  pl.BlockSpec((1, d), lambda i: (0, 0))],
        out_specs=pl.BlockSpec((tm, d), lambda i: (i, 0)),
        out_shape=jax.ShapeDtypeStruct((m, d), F32),
        compiler_params=_params("parallel"),
        name="final_rms_norm",
    )(h, gain.reshape(1, d))


def _conv_glu_ffn(h, gain, w_up, w_conv, b_conv, w_down, seq):
    act = ffn_up(h, gain, w_up.astype(BF16), w_conv, b_conv, seq)
    return matmul_residual(act, w_down.astype(BF16), h, tm_target=512, tn_target=256)


def kernel(x, a_attn_norm, a_w_qkv, a_w_o, a_ffn_norm, a_w_up, a_w_conv, a_b_conv, a_w_down,
           kv_norm, w_kv, w_f, b_f,
           b_attn_norm, b_w_q, b_w_o, b_ffn_norm, b_w_up, b_w_conv, b_b_conv, b_w_down,
           final_norm):
    bsz, seq, d = x.shape
    m = bsz * seq
    n_heads = d // HEAD_DIM
    n_kv_heads = w_kv.shape[1] // (2 * HEAD_DIM)
    group = n_heads // n_kv_heads
    n_gates = w_f.shape[1]
    assert n_gates == n_heads and n_gates <= V7X_LANES

    h = x.reshape(m, d)

    for i in range(a_w_qkv.shape[0]):
        qkv = norm_matmul(h, a_attn_norm[i], a_w_qkv[i].astype(BF16))
        o = stick_breaking_attention(qkv.reshape(bsz, seq, 3 * d), n_heads)
        h = matmul_residual(o.reshape(m, d), a_w_o[i].astype(BF16), h, tm_target=1024, tn_target=512)
        h = _conv_glu_ffn(h, a_ffn_norm[i], a_w_up[i], a_w_conv[i], a_b_conv[i], a_w_down[i], seq)

    kv = c_col = c_row = None
    for j in range(b_w_q.shape[0]):
        if j == 0:
            wf_pad = jnp.pad(w_f, ((0, 0), (0, V7X_LANES - n_gates))).astype(BF16)
            bf_pad = jnp.pad(b_f, (0, V7X_LANES - n_gates)).reshape(1, V7X_LANES)
            kv, log_f = kv_proj(h, kv_norm, w_kv.astype(BF16), wf_pad, bf_pad)
            kv = kv.reshape(bsz, seq, 2 * n_kv_heads * HEAD_DIM)
            c_col = cumsum_positions(log_f.reshape(bsz, seq, V7X_LANES))
            c_row = (c_col[:, :, :n_gates].transpose(0, 2, 1)
                     .reshape(bsz, n_kv_heads, group, seq))
        q = norm_matmul(h, b_attn_norm[j], b_w_q[j].astype(BF16))
        o = forgetting_attention(q.reshape(bsz, seq, d), kv, c_col, c_row, n_kv_heads)
        h = matmul_residual(o.reshape(m, d), b_w_o[j].astype(BF16), h, tm_target=1024, tn_target=512)
        h = _conv_glu_ffn(h, b_ffn_norm[j], b_w_up[j], b_w_conv[j], b_b_conv[j], b_w_down[j], seq)

    return rms_norm(h, final_norm).reshape(bsz, seq, d)
```

```python
import functools

import jax
import jax.numpy as jnp
from jax import lax
from jax.experimental import pallas as pl
from jax.experimental.pallas import tpu as pltpu

HEAD_DIM = 128
CONV_WIDTH = 3
RMS_EPS = 1e-6

V7X_LANES = 128
V7X_SUBLANES = 8
V7X_MXU_DIM = 256
V7X_VMEM_LIMIT_BYTES = 56 * 1024 * 1024

F32 = jnp.float32
BF16 = jnp.bfloat16

NORM_ROW_CHUNK = 32
ATTN_ROW_CHUNK = 32
MASK_VALUE = -1e30
LOG2_E = 1.4426950408889634
F32_SIGN_BIT = 0x80000000


def _params(*semantics):
    return pltpu.CompilerParams(dimension_semantics=semantics,
                                vmem_limit_bytes=V7X_VMEM_LIMIT_BYTES)


def _pick_tile(n, target, quantum):
    if n <= target:
        return n
    t = (target // quantum) * quantum
    while t >= quantum:
        if n % t == 0:
            return t
        t -= quantum
    raise ValueError(f"no tile for {n}")


def _rms_to_scratch(h_ref, g_ref, xn_ref):
    tm = h_ref.shape[0]

    def body(c, carry):
        r0 = pl.multiple_of(c * NORM_ROW_CHUNK, NORM_ROW_CHUNK)
        x = h_ref[pl.ds(r0, NORM_ROW_CHUNK), :]
        ms = jnp.mean(x * x, axis=-1, keepdims=True)
        y = x * lax.rsqrt(ms + RMS_EPS) * g_ref[...]
        xn_ref[pl.ds(r0, NORM_ROW_CHUNK), :] = y.astype(xn_ref.dtype)
        return carry

    lax.fori_loop(0, tm // NORM_ROW_CHUNK, body, 0, unroll=2)


def _norm_matmul_kernel(h_ref, g_ref, w_ref, o_ref, xn_ref):
    @pl.when(pl.program_id(1) == 0)
    def _():
        _rms_to_scratch(h_ref, g_ref, xn_ref)

    o_ref[...] = jnp.dot(xn_ref[...], w_ref[...],
                         preferred_element_type=F32).astype(o_ref.dtype)


def norm_matmul(h, gain, w, *, tm_target=1024, tn_target=512):
    m, d = h.shape
    n = w.shape[1]
    tm = _pick_tile(m, tm_target, NORM_ROW_CHUNK)
    tn = _pick_tile(n, tn_target, V7X_LANES)
    return pl.pallas_call(
        _norm_matmul_kernel,
        grid=(m // tm, n // tn),
        in_specs=[
            pl.BlockSpec((tm, d), lambda i, j: (i, 0)),
            pl.BlockSpec((1, d), lambda i, j: (0, 0)),
            pl.BlockSpec((d, tn), lambda i, j: (0, j)),
        ],
        out_specs=pl.BlockSpec((tm, tn), lambda i, j: (i, j)),
        out_shape=jax.ShapeDtypeStruct((m, n), BF16),
        scratch_shapes=[pltpu.VMEM((tm, d), BF16)],
        compiler_params=_params("parallel", "arbitrary"),
        name="norm_matmul",
    )(h, gain.reshape(1, d), w)


def _kv_proj_kernel(h_ref, g_ref, w_ref, wf_ref, bf_ref, o_ref, lf_ref, xn_ref):
    @pl.when(pl.program_id(1) == 0)
    def _():
        _rms_to_scratch(h_ref, g_ref, xn_ref)
        pre = jnp.dot(xn_ref[...], wf_ref[...], preferred_element_type=F32) + bf_ref[...]
        lf_ref[...] = -(jnp.maximum(-pre, 0.0) + jnp.log(1.0 + jnp.exp(-jnp.abs(pre))))

    o_ref[...] = jnp.dot(xn_ref[...], w_ref[...],
                         preferred_element_type=F32).astype(o_ref.dtype)


def kv_proj(h, gain, w_kv, wf_pad, bf_pad, *, tm_target=1024, tn_target=512):
    m, d = h.shape
    n = w_kv.shape[1]
    nf = wf_pad.shape[1]
    tm = _pick_tile(m, tm_target, NORM_ROW_CHUNK)
    tn = _pick_tile(n, tn_target, V7X_LANES)
    return pl.pallas_call(
        _kv_proj_kernel,
        grid=(m // tm, n // tn),
        in_specs=[
            pl.BlockSpec((tm, d), lambda i, j: (i, 0)),
            pl.BlockSpec((1, d), lambda i, j: (0, 0)),
            pl.BlockSpec((d, tn), lambda i, j: (0, j)),
            pl.BlockSpec((d, nf), lambda i, j: (0, 0)),
            pl.BlockSpec((1, nf), lambda i, j: (0, 0)),
        ],
        out_specs=[
            pl.BlockSpec((tm, tn), lambda i, j: (i, j)),
            pl.BlockSpec((tm, nf), lambda i, j: (i, 0)),
        ],
        out_shape=[
            jax.ShapeDtypeStruct((m, n), BF16),
            jax.ShapeDtypeStruct((m, nf), F32),
        ],
        scratch_shapes=[pltpu.VMEM((tm, d), BF16)],
        compiler_params=_params("parallel", "arbitrary"),
        name="kv_proj",
    )(h, gain.reshape(1, d), w_kv, wf_pad, bf_pad)


def _split3(x):
    hi = x.astype(BF16)
    r1 = x - hi.astype(F32)
    mid = r1.astype(BF16)
    lo = (r1 - mid.astype(F32)).astype(BF16)
    return hi, mid, lo


def _cumsum_kernel(x_ref, tri_ref, c_ref, run_ref):
    blk = tri_ref.shape[0]
    nblk = x_ref.shape[0] // blk
    run_ref[...] = jnp.zeros_like(run_ref)

    def body(n, carry):
        r0 = pl.multiple_of(n * blk, blk)
        hi, mid, lo = _split3(x_ref[pl.ds(r0, blk), :])
        tri = tri_ref[...]
        incl = (jnp.dot(tri, hi, preferred_element_type=F32)
                + jnp.dot(tri, mid, preferred_element_type=F32)
                + jnp.dot(tri, lo, preferred_element_type=F32)
                + run_ref[0:1, :])
        c_ref[pl.ds(r0, blk), :] = incl
        run_ref[0:1, :] = incl[blk - 1:blk, :]
        return carry

    lax.fori_loop(0, nblk, body, 0)


def cumsum_positions(x):
    b, s, c = x.shape
    blk = V7X_LANES
    tri = (lax.broadcasted_iota(jnp.int32, (blk, blk), 1)
           <= lax.broadcasted_iota(jnp.int32, (blk, blk), 0)).astype(BF16)
    return pl.pallas_call(
        _cumsum_kernel,
        grid=(b,),
        in_specs=[
            pl.BlockSpec((None, s, c), lambda i: (i, 0, 0)),
            pl.BlockSpec((blk, blk), lambda i: (0, 0)),
        ],
        out_specs=pl.BlockSpec((None, s, c), lambda i: (i, 0, 0)),
        out_shape=jax.ShapeDtypeStruct((b, s, c), F32),
        scratch_shapes=[pltpu.VMEM((V7X_SUBLANES, c), F32)],
        compiler_params=_params("parallel"),
        name="cumsum_positions",
    )(x, tri)


def _matmul_residual_kernel(a_ref, w_ref, r_ref, o_ref):
    o_ref[...] = r_ref[...] + jnp.dot(a_ref[...], w_ref[...], preferred_element_type=F32)


def matmul_residual(a, w, res, *, tm_target, tn_target):
    m, k = a.shape
    n = w.shape[1]
    tm = _pick_tile(m, tm_target, 2 * V7X_SUBLANES)
    tn = _pick_tile(n, tn_target, V7X_LANES)
    return pl.pallas_call(
        _matmul_residual_kernel,
        grid=(m // tm, n // tn),
        in_specs=[
            pl.BlockSpec((tm, k), lambda i, j: (i, 0)),
            pl.BlockSpec((k, tn), lambda i, j: (0, j)),
            pl.BlockSpec((tm, tn), lambda i, j: (i, j)),
        ],
        out_specs=pl.BlockSpec((tm, tn), lambda i, j: (i, j)),
        out_shape=jax.ShapeDtypeStruct((m, n), F32),
        compiler_params=_params("parallel", "parallel"),
        name="matmul_residual",
    )(a, w, res)


def _ffn_up_kernel(h_ref, g_ref, wg_ref, wv_ref, cp_ref, o_ref,
                   xn_ref, halo_ref, tail_ref, *, tiles_per_seq, row_splits):
    i = pl.program_id(0)
    j = pl.program_id(1)
    tm, tn = o_ref.shape
    halo = V7X_SUBLANES
    mc = tm // row_splits

    @pl.when(j == 0)
    def _():
        _rms_to_scratch(h_ref, g_ref, xn_ref)

    seq_start = (i % tiles_per_seq) == 0

    @pl.when(seq_start)
    def _():
        halo_ref[...] = jnp.zeros_like(halo_ref)

    @pl.when(jnp.logical_not(seq_start))
    def _():
        halo_ref[...] = tail_ref[j]

    cp = cp_ref[j]
    prev_rows = halo_ref[...]
    for c in range(row_splits):
        rows = slice(c * mc, (c + 1) * mc)
        xc = xn_ref[rows, :]
        gate = jnp.dot(xc, wg_ref[...], preferred_element_type=F32)
        val = jnp.dot(xc, wv_ref[...], preferred_element_type=F32)
        ext = jnp.concatenate([prev_rows, gate], axis=0)
        g1 = pltpu.roll(ext, 1, 0)[halo:, :]
        g2 = pltpu.roll(ext, 2, 0)[halo:, :]
        conv = cp[3:4, :] + (cp[0:1, :] * g2 + cp[1:2, :] * g1 + cp[2:3, :] * gate)
        act = conv * (1.0 / (1.0 + jnp.exp(-conv))) * val
        o_ref[rows, :] = act.astype(o_ref.dtype)
        prev_rows = gate[mc - halo:mc, :]
    tail_ref[j] = prev_rows


def ffn_up(h, gain, w_up, w_conv, b_conv, seq, *, tm_target=1024, tn_target=256):
    m, d = h.shape
    f = w_up.shape[1] // 2
    tm = _pick_tile(seq, tm_target, NORM_ROW_CHUNK)
    tn = _pick_tile(f, tn_target, V7X_LANES)
    nj = f // tn
    halo = V7X_SUBLANES
    row_splits = 4 if tm % (4 * NORM_ROW_CHUNK) == 0 else 1
    assert tm // row_splits >= halo and CONV_WIDTH - 1 <= halo and CONV_WIDTH + 1 <= V7X_SUBLANES
    conv_params = jnp.concatenate(
        [w_conv, b_conv.reshape(1, f),
         jnp.zeros((V7X_SUBLANES - CONV_WIDTH - 1, f), F32)], axis=0)
    conv_params = conv_params.reshape(V7X_SUBLANES, nj, tn).transpose(1, 0, 2)
    return pl.pallas_call(
        functools.partial(_ffn_up_kernel, tiles_per_seq=seq // tm, row_splits=row_splits),
        grid=(m // tm, nj),
        in_specs=[
            pl.BlockSpec((tm, d), lambda i, j: (i, 0)),
            pl.BlockSpec((1, d), lambda i, j: (0, 0)),
            pl.BlockSpec((d, tn), lambda i, j: (0, j)),
            pl.BlockSpec((d, tn), lambda i, j: (0, j + nj)),
            pl.BlockSpec((nj, V7X_SUBLANES, tn), lambda i, j: (0, 0, 0)),
        ],
        out_specs=pl.BlockSpec((tm, tn), lambda i, j: (i, j)),
        out_shape=jax.ShapeDtypeStruct((m, f), BF16),
        scratch_shapes=[
            pltpu.VMEM((tm, d), BF16),
            pltpu.VMEM((halo, tn), F32),
            pltpu.VMEM((nj, halo, tn), F32),
        ],
        compiler_params=_params("arbitrary", "arbitrary"),
        name="ffn_up",
    )(h, gain.reshape(1, d), w_up, w_up, conv_params)


def _sb_attn_kernel(q_ref, k_ref, v_ref, u2_ref, o_ref,
                    acc_ref, run_ref, z_ref, hl_ref, incl_ref, w_ref, *, tq, tk, heads, scale):
    seq = q_ref.shape[0]
    nsub = tq // tk
    reps = tk // V7X_LANES
    score_scale = scale * LOG2_E
    nchunk = tq // ATTN_ROW_CHUNK

    def strict_mask(r):
        shape = (ATTN_ROW_CHUNK, tq)
        return (lax.broadcasted_iota(jnp.int32, shape, 1)
                < r * ATTN_ROW_CHUNK + lax.broadcasted_iota(jnp.int32, shape, 0))

    def one_block(q0, kb, masked):
        k0 = pl.multiple_of(kb * tq, tq)
        for hh in range(heads):
            cols_h = slice(hh * HEAD_DIM, (hh + 1) * HEAD_DIM)
            z_ref[hh] = lax.dot_general(q_ref[pl.ds(q0, tq), cols_h], k_ref[pl.ds(k0, tq), cols_h],
                                        (((1,), (1,)), ((), ())), preferred_element_type=F32)
            for r in range(nchunk):
                rows = slice(r * ATTN_ROW_CHUNK, (r + 1) * ATTN_ROW_CHUNK)
                nz2 = z_ref[hh, rows, :] * (-score_scale)
                z_ref[hh, rows, :] = nz2
                neg_abs = lax.bitcast_convert_type(
                    lax.bitcast_convert_type(nz2, jnp.uint32) | jnp.uint32(F32_SIGN_BIT), F32)
                lsm = jnp.minimum(nz2, 0.0) - jnp.log2(1.0 + jnp.exp2(neg_abs))
                if masked:
                    lsm = jnp.where(strict_mask(r), lsm, 0.0)
                for c in range(nsub):
                    sub = lsm[:, c * tk:(c + 1) * tk]
                    hi = sub.astype(BF16)
                    hl_ref[hh, c, rows, 0:tk] = hi
                    hl_ref[hh, c, rows, tk:2 * tk] = (sub - hi.astype(F32)).astype(BF16)
            for c in range(nsub):
                incl_ref[hh, c] = jnp.dot(hl_ref[hh, c], u2_ref[...], preferred_element_type=F32)
            for r in range(nchunk):
                rows = slice(r * ATTN_ROW_CHUNK, (r + 1) * ATTN_ROW_CHUNK)
                run = run_ref[hh, rows, :]
                for c in reversed(range(nsub)):
                    cols = slice(c * tk, (c + 1) * tk)
                    incl = incl_ref[hh, c, rows, :]
                    e2 = incl + jnp.tile(run, (1, reps)) - z_ref[hh, rows, cols]
                    w = jnp.exp2(e2)
                    if masked:
                        w = jnp.where(strict_mask(r)[:, cols], w, 0.0)
                    w_ref[hh, rows, cols] = w.astype(BF16)
                    run = run + jnp.broadcast_to(incl[:, 0:1], run.shape)
                run_ref[hh, rows, :] = run
            acc_ref[hh] += jnp.dot(w_ref[hh], v_ref[pl.ds(k0, tq), cols_h],
                                   preferred_element_type=F32)

    def q_body(qi, carry):
        q0 = pl.multiple_of(qi * tq, tq)
        acc_ref[...] = jnp.zeros_like(acc_ref)
        run_ref[...] = jnp.zeros_like(run_ref)
        one_block(q0, qi, True)

        def kb_body(n, c):
            one_block(q0, qi - 1 - n, False)
            return c

        lax.fori_loop(0, qi, kb_body, 0)
        for hh in range(heads):
            o_ref[pl.ds(q0, tq), hh * HEAD_DIM:(hh + 1) * HEAD_DIM] = acc_ref[hh].astype(o_ref.dtype)
        return carry

    lax.fori_loop(0, seq // tq, q_body, 0)


def stick_breaking_attention(qkv, n_heads, *, tq_target=512, tk=V7X_MXU_DIM, heads_per_step=4):
    b, s, _ = qkv.shape
    tk = min(tk, s)
    tq = _pick_tile(s, tq_target, tk)
    hps = heads_per_step if n_heads % heads_per_step == 0 else 1
    nstep = n_heads // hps
    width = hps * HEAD_DIM
    u = (lax.broadcasted_iota(jnp.int32, (tk, tk), 0)
         >= lax.broadcasted_iota(jnp.int32, (tk, tk), 1)).astype(BF16)
    u2 = jnp.concatenate([u, u], axis=0)
    kern = functools.partial(_sb_attn_kernel, tq=tq, tk=tk, heads=hps, scale=HEAD_DIM ** -0.5)
    return pl.pallas_call(
        kern,
        grid=(b, nstep),
        in_specs=[
            pl.BlockSpec((None, s, width), lambda bi, h: (bi, 0, h)),
            pl.BlockSpec((None, s, width), lambda bi, h: (bi, 0, nstep + h)),
            pl.BlockSpec((None, s, width), lambda bi, h: (bi, 0, 2 * nstep + h)),
            pl.BlockSpec((2 * tk, tk), lambda bi, h: (0, 0)),
        ],
        out_specs=pl.BlockSpec((None, s, width), lambda bi, h: (bi, 0, h)),
        out_shape=jax.ShapeDtypeStruct((b, s, n_heads * HEAD_DIM), BF16),
        scratch_shapes=[
            pltpu.VMEM((hps, tq, HEAD_DIM), F32),
            pltpu.VMEM((hps, tq, V7X_LANES), F32),
            pltpu.VMEM((hps, tq, tq), F32),
            pltpu.VMEM((hps, tq // tk, tq, 2 * tk), BF16),
            pltpu.VMEM((hps, tq // tk, tq, tk), F32),
            pltpu.VMEM((hps, tq, tq), BF16),
        ],
        compiler_params=_params("parallel", "parallel"),
        name="stick_breaking_attention",
    )(qkv, qkv, qkv, u2)


def _fox_attn_kernel(q_ref, k_ref, v_ref, ccol_ref, crow_ref, o_ref,
                     acc_ref, m_ref, l_ref, bt_ref, *, tq, group, scale):
    seq = k_ref.shape[0]
    kvh = pl.program_id(1)
    reps = tq // V7X_LANES
    score_scale = scale * LOG2_E

    def one_block(q0, kb, masked):
        k0 = pl.multiple_of(kb * tq, tq)
        kblk = k_ref[pl.ds(k0, tq), :]
        vblk = v_ref[pl.ds(k0, tq), :]
        for g in range(group):
            heads = slice(g * HEAD_DIM, (g + 1) * HEAD_DIM)
            s_val = lax.dot_general(q_ref[pl.ds(q0, tq), heads], kblk,
                                    (((1,), (1,)), ((), ())), preferred_element_type=F32)
            cs = crow_ref[g:g + 1, pl.ds(k0, tq)] * LOG2_E
            p_parts = []
            for r in range(tq // ATTN_ROW_CHUNK):
                rows = slice(r * ATTN_ROW_CHUNK, (r + 1) * ATTN_ROW_CHUNK)
                grows = slice(g * tq + rows.start, g * tq + rows.stop)
                u = s_val[rows, :] * score_scale - cs
                if masked:
                    shape = (ATTN_ROW_CHUNK, tq)
                    causal = (lax.broadcasted_iota(jnp.int32, shape, 1)
                              <= rows.start + lax.broadcasted_iota(jnp.int32, shape, 0))
                    u = jnp.where(causal, u, MASK_VALUE)
                bt = bt_ref[grows, :]
                m_old = m_ref[grows, :]
                m_new = jnp.maximum(m_old, jnp.max(u, axis=-1, keepdims=True) + bt)
                alpha = jnp.exp2(m_old - m_new)
                p = jnp.exp2(u - jnp.tile(m_new - bt, (1, reps)))
                l_ref[grows, :] = alpha * l_ref[grows, :] + jnp.sum(p, axis=-1, keepdims=True)
                m_ref[grows, :] = m_new
                acc_ref[grows, :] = alpha * acc_ref[grows, :]
                p_parts.append(p.astype(BF16))
            acc_ref[g * tq:(g + 1) * tq, :] += jnp.dot(jnp.concatenate(p_parts, axis=0), vblk,
                                                       preferred_element_type=F32)

    def q_body(qi, carry):
        q0 = pl.multiple_of(qi * tq, tq)
        cblk = ccol_ref[pl.ds(q0, tq), :]
        lane = lax.broadcasted_iota(jnp.int32, cblk.shape, 1)
        for g in range(group):
            c_t = jnp.sum(jnp.where(lane == kvh * group + g, cblk, 0.0), axis=-1, keepdims=True)
            bt_ref[g * tq:(g + 1) * tq, :] = jnp.broadcast_to(c_t * LOG2_E, (tq, V7X_LANES))
        acc_ref[...] = jnp.zeros_like(acc_ref)
        l_ref[...] = jnp.zeros_like(l_ref)
        m_ref[...] = jnp.full_like(m_ref, MASK_VALUE)
        one_block(q0, qi, True)

        def kb_body(kb, c):
            one_block(q0, kb, False)
            return c

        lax.fori_loop(0, qi, kb_body, 0)
        out = acc_ref[...] / l_ref[...]
        for g in range(group):
            o_ref[pl.ds(q0, tq), g * HEAD_DIM:(g + 1) * HEAD_DIM] = (
                out[g * tq:(g + 1) * tq, :].astype(o_ref.dtype))
        return carry

    lax.fori_loop(0, seq // tq, q_body, 0)


def forgetting_attention(q, kv, c_col, c_row, n_kv_heads, *, tq_target=512):
    b, s, dq = q.shape
    assert HEAD_DIM == V7X_LANES
    group = dq // HEAD_DIM // n_kv_heads
    tq = _pick_tile(s, tq_target, V7X_LANES)
    cw = c_col.shape[2]
    kern = functools.partial(_fox_attn_kernel, tq=tq, group=group, scale=HEAD_DIM ** -0.5)
    stat = pltpu.VMEM((group * tq, V7X_LANES), F32)
    return pl.pallas_call(
        kern,
        grid=(b, n_kv_heads),
        in_specs=[
            pl.BlockSpec((None, s, group * HEAD_DIM), lambda bi, h: (bi, 0, h)),
            pl.BlockSpec((None, s, HEAD_DIM), lambda bi, h: (bi, 0, h)),
            pl.BlockSpec((None, s, HEAD_DIM), lambda bi, h: (bi, 0, n_kv_heads + h)),
            pl.BlockSpec((None, s, cw), lambda bi, h: (bi, 0, 0)),
            pl.BlockSpec((None, None, group, s), lambda bi, h: (bi, h, 0, 0)),
        ],
        out_specs=pl.BlockSpec((None, s, group * HEAD_DIM), lambda bi, h: (bi, 0, h)),
        out_shape=jax.ShapeDtypeStruct((b, s, dq), BF16),
        scratch_shapes=[stat, stat, stat, stat],
        compiler_params=_params("parallel", "parallel"),
        name="forgetting_attention",
    )(q, kv, kv, c_col, c_row)


def _rms_norm_kernel(h_ref, g_ref, o_ref):
    x = h_ref[...]
    ms = jnp.mean(x * x, axis=-1, keepdims=True)
    o_ref[...] = x * lax.rsqrt(ms + RMS_EPS) * g_ref[...]


def rms_norm(h, gain, *, tm_target=256):
    m, d = h.shape
    tm = _pick_tile(m, tm_target, V7X_SUBLANES)
    return pl.pallas_call(
        _rms_norm_kernel,
        grid=(m // tm,),
        in_specs=[pl.BlockSpec((tm, d), lambda i: (i, 0)),
                  pl.BlockSpec((1, d), lambda i: (0, 0))],
        out_specs=pl.BlockSpec((tm, d), lambda i: (i, 0)),
        out_shape=jax.ShapeDtypeStruct((m, d), F32),
        compiler_params=_params("parallel"),
        name="final_rms_norm",
    )(h, gain.reshape(1, d))


def _conv_glu_ffn(h, gain, w_up, w_conv, b_conv, w_down, seq):
    act = ffn_up(h, gain, w_up.astype(BF16), w_conv, b_conv, seq)
    return matmul_residual(act, w_down.astype(BF16), h, tm_target=512, tn_target=512)


def kernel(x, a_attn_norm, a_w_qkv, a_w_o, a_ffn_norm, a_w_up, a_w_conv, a_b_conv, a_w_down,
           kv_norm, w_kv, w_f, b_f,
           b_attn_norm, b_w_q, b_w_o, b_ffn_norm, b_w_up, b_w_conv, b_b_conv, b_w_down,
           final_norm):
    bsz, seq, d = x.shape
    m = bsz * seq
    n_heads = d // HEAD_DIM
    n_kv_heads = w_kv.shape[1] // (2 * HEAD_DIM)
    group = n_heads // n_kv_heads
    n_gates = w_f.shape[1]
    assert n_gates == n_heads and n_gates <= V7X_LANES

    h = x.reshape(m, d)

    for i in range(a_w_qkv.shape[0]):
        qkv = norm_matmul(h, a_attn_norm[i], a_w_qkv[i].astype(BF16))
        o = stick_breaking_attention(qkv.reshape(bsz, seq, 3 * d), n_heads)
        h = matmul_residual(o.reshape(m, d), a_w_o[i].astype(BF16), h, tm_target=1024, tn_target=1024)
        h = _conv_glu_ffn(h, a_ffn_norm[i], a_w_up[i], a_w_conv[i], a_b_conv[i], a_w_down[i], seq)

    kv = c_col = c_row = None
    for j in range(b_w_q.shape[0]):
        if j == 0:
            wf_pad = jnp.pad(w_f, ((0, 0), (0, V7X_LANES - n_gates))).astype(BF16)
            bf_pad = jnp.pad(b_f, (0, V7X_LANES - n_gates)).reshape(1, V7X_LANES)
            kv, log_f = kv_proj(h, kv_norm, w_kv.astype(BF16), wf_pad, bf_pad)
            kv = kv.reshape(bsz, seq, 2 * n_kv_heads * HEAD_DIM)
            c_col = cumsum_positions(log_f.reshape(bsz, seq, V7X_LANES))
            c_row = (c_col[:, :, :n_gates].transpose(0, 2, 1)
                     .reshape(bsz, n_kv_heads, group, seq))
        q = norm_matmul(h, b_attn_norm[j], b_w_q[j].astype(BF16))
        o = forgetting_attention(q.reshape(bsz, seq, d), kv, c_col, c_row, n_kv_heads)
        h = matmul_residual(o.reshape(m, d), b_w_o[j].astype(BF16), h, tm_target=1024, tn_target=1024)
        h = _conv_glu_ffn(h, b_ffn_norm[j], b_w_up[j], b_w_conv[j], b_b_conv[j], b_w_down[j], seq)

    return rms_norm(h, final_norm).reshape(bsz, seq, d)
```

```python
import functools

import jax
import jax.numpy as jnp
from jax import lax
from jax.experimental import pallas as pl
from jax.experimental.pallas import tpu as pltpu

HEAD_DIM = 128
CONV_WIDTH = 3
RMS_EPS = 1e-6

V7X_LANES = 128
V7X_SUBLANES = 8
V7X_MXU_DIM = 256
V7X_VMEM_LIMIT_BYTES = 56 * 1024 * 1024

F32 = jnp.float32
BF16 = jnp.bfloat16

NORM_ROW_CHUNK = 32
ATTN_ROW_CHUNK = 32
MASK_VALUE = -1e30
LOG2_E = 1.4426950408889634
F32_SIGN_BIT = 0x80000000


def _params(*semantics):
    return pltpu.CompilerParams(dimension_semantics=semantics,
                                vmem_limit_bytes=V7X_VMEM_LIMIT_BYTES)


def _pick_tile(n, target, quantum):
    if n <= target:
        return n
    t = (target // quantum) * quantum
    while t >= quantum:
        if n % t == 0:
            return t
        t -= quantum
    raise ValueError(f"no tile for {n}")


def _rms_norm_cast_kernel(h_ref, g_ref, *xn_refs):
    tm = h_ref.shape[0]

    def body(c, carry):
        r0 = pl.multiple_of(c * NORM_ROW_CHUNK, NORM_ROW_CHUNK)
        x = h_ref[pl.ds(r0, NORM_ROW_CHUNK), :]
        ms = jnp.mean(x * x, axis=-1, keepdims=True)
        y = x * lax.rsqrt(ms + RMS_EPS)
        for k, xn_ref in enumerate(xn_refs):
            xn_ref[pl.ds(r0, NORM_ROW_CHUNK), :] = (y * g_ref[k:k + 1, :]).astype(xn_ref.dtype)
        return carry

    lax.fori_loop(0, tm // NORM_ROW_CHUNK, body, 0, unroll=2)


def rms_norm_cast(h, gains, *, tm_target=256):
    m, d = h.shape
    ng = len(gains)
    tm = _pick_tile(m, tm_target, 2 * NORM_ROW_CHUNK)
    out = pl.pallas_call(
        _rms_norm_cast_kernel,
        grid=(m // tm,),
        in_specs=[pl.BlockSpec((tm, d), lambda i: (i, 0)),
                  pl.BlockSpec((ng, d), lambda i: (0, 0))],
        out_specs=[pl.BlockSpec((tm, d), lambda i: (i, 0))] * ng,
        out_shape=[jax.ShapeDtypeStruct((m, d), BF16)] * ng,
        compiler_params=_params("parallel"),
        name="rms_norm_cast",
    )(h, jnp.stack(gains))
    return list(out)


def _matmul_kernel(x_ref, w_ref, o_ref):
    o_ref[...] = jnp.dot(x_ref[...], w_ref[...], preferred_element_type=F32).astype(o_ref.dtype)


def matmul(x, w, *, tm_target=2048, tn_target=512):
    m, k = x.shape
    n = w.shape[1]
    tm = _pick_tile(m, tm_target, 2 * V7X_SUBLANES)
    tn = _pick_tile(n, tn_target, V7X_LANES)
    return pl.pallas_call(
        _matmul_kernel,
        grid=(m // tm, n // tn),
        in_specs=[
            pl.BlockSpec((tm, k), lambda i, j: (i, 0)),
            pl.BlockSpec((k, tn), lambda i, j: (0, j)),
        ],
        out_specs=pl.BlockSpec((tm, tn), lambda i, j: (i, j)),
        out_shape=jax.ShapeDtypeStruct((m, n), BF16),
        compiler_params=_params("parallel", "parallel"),
        name="matmul",
    )(x, w)


def _kv_proj_kernel(x_ref, w_ref, wf_ref, bf_ref, o_ref, lf_ref):
    @pl.when(pl.program_id(1) == 0)
    def _():
        pre = jnp.dot(x_ref[...], wf_ref[...], preferred_element_type=F32) + bf_ref[...]
        lf_ref[...] = -(jnp.maximum(-pre, 0.0) + jnp.log(1.0 + jnp.exp(-jnp.abs(pre))))

    o_ref[...] = jnp.dot(x_ref[...], w_ref[...], preferred_element_type=F32).astype(o_ref.dtype)


def kv_proj(x, w_kv, wf_pad, bf_pad, *, tm_target=1024, tn_target=1024):
    m, d = x.shape
    n = w_kv.shape[1]
    nf = wf_pad.shape[1]
    tm = _pick_tile(m, tm_target, 2 * V7X_SUBLANES)
    tn = _pick_tile(n, tn_target, V7X_LANES)
    return pl.pallas_call(
        _kv_proj_kernel,
        grid=(m // tm, n // tn),
        in_specs=[
            pl.BlockSpec((tm, d), lambda i, j: (i, 0)),
            pl.BlockSpec((d, tn), lambda i, j: (0, j)),
            pl.BlockSpec((d, nf), lambda i, j: (0, 0)),
            pl.BlockSpec((1, nf), lambda i, j: (0, 0)),
        ],
        out_specs=[
            pl.BlockSpec((tm, tn), lambda i, j: (i, j)),
            pl.BlockSpec((tm, nf), lambda i, j: (i, 0)),
        ],
        out_shape=[
            jax.ShapeDtypeStruct((m, n), BF16),
            jax.ShapeDtypeStruct((m, nf), F32),
        ],
        compiler_params=_params("parallel", "arbitrary"),
        name="kv_proj",
    )(x, w_kv, wf_pad, bf_pad)


def _split3(x):
    hi = x.astype(BF16)
    r1 = x - hi.astype(F32)
    mid = r1.astype(BF16)
    lo = (r1 - mid.astype(F32)).astype(BF16)
    return hi, mid, lo


def _cumsum_kernel(x_ref, tri_ref, c_ref, run_ref):
    blk = tri_ref.shape[0]
    nblk = x_ref.shape[0] // blk
    run_ref[...] = jnp.zeros_like(run_ref)

    def body(n, carry):
        r0 = pl.multiple_of(n * blk, blk)
        hi, mid, lo = _split3(x_ref[pl.ds(r0, blk), :])
        tri = tri_ref[...]
        incl = (jnp.dot(tri, hi, preferred_element_type=F32)
                + jnp.dot(tri, mid, preferred_element_type=F32)
                + jnp.dot(tri, lo, preferred_element_type=F32)
                + run_ref[0:1, :])
        c_ref[pl.ds(r0, blk), :] = incl
        run_ref[0:1, :] = incl[blk - 1:blk, :]
        return carry

    lax.fori_loop(0, nblk, body, 0)


def cumsum_positions(x):
    b, s, c = x.shape
    blk = V7X_LANES
    tri = (lax.broadcasted_iota(jnp.int32, (blk, blk), 1)
           <= lax.broadcasted_iota(jnp.int32, (blk, blk), 0)).astype(BF16)
    return pl.pallas_call(
        _cumsum_kernel,
        grid=(b,),
        in_specs=[
            pl.BlockSpec((None, s, c), lambda i: (i, 0, 0)),
            pl.BlockSpec((blk, blk), lambda i: (0, 0)),
        ],
        out_specs=pl.BlockSpec((None, s, c), lambda i: (i, 0, 0)),
        out_shape=jax.ShapeDtypeStruct((b, s, c), F32),
        scratch_shapes=[pltpu.VMEM((V7X_SUBLANES, c), F32)],
        compiler_params=_params("parallel"),
        name="cumsum_positions",
    )(x, tri)


def _matmul_residual_kernel(a_ref, w_ref, r_ref, o_ref):
    o_ref[...] = r_ref[...] + jnp.dot(a_ref[...], w_ref[...], preferred_element_type=F32)


def matmul_residual(a, w, res, *, tm_target, tn_target):
    m, k = a.shape
    n = w.shape[1]
    tm = _pick_tile(m, tm_target, 2 * V7X_SUBLANES)
    tn = _pick_tile(n, tn_target, V7X_LANES)
    return pl.pallas_call(
        _matmul_residual_kernel,
        grid=(m // tm, n // tn),
        in_specs=[
            pl.BlockSpec((tm, k), lambda i, j: (i, 0)),
            pl.BlockSpec((k, tn), lambda i, j: (0, j)),
            pl.BlockSpec((tm, tn), lambda i, j: (i, j)),
        ],
        out_specs=pl.BlockSpec((tm, tn), lambda i, j: (i, j)),
        out_shape=jax.ShapeDtypeStruct((m, n), F32),
        compiler_params=_params("parallel", "parallel"),
        name="matmul_residual",
    )(a, w, res)


def _ffn_up_kernel(xn_ref, wg_ref, wv_ref, cp_ref, o_ref, halo_ref, tail_ref,
                   *, tiles_per_seq, row_splits):
    i = pl.program_id(0)
    j = pl.program_id(1)
    tm, tn = o_ref.shape
    halo = V7X_SUBLANES
    mc = tm // row_splits

    seq_start = (i % tiles_per_seq) == 0

    @pl.when(seq_start)
    def _():
        halo_ref[...] = jnp.zeros_like(halo_ref)

    @pl.when(jnp.logical_not(seq_start))
    def _():
        halo_ref[...] = tail_ref[j]

    cp = cp_ref[j]
    prev_rows = halo_ref[...]
    for c in range(row_splits):
        rows = slice(c * mc, (c + 1) * mc)
        xc = xn_ref[rows, :]
        gate = jnp.dot(xc, wg_ref[...], preferred_element_type=F32)
        val = jnp.dot(xc, wv_ref[...], preferred_element_type=F32)
        ext = jnp.concatenate([prev_rows, gate], axis=0)
        g1 = pltpu.roll(ext, 1, 0)[halo:, :]
        g2 = pltpu.roll(ext, 2, 0)[halo:, :]
        conv = cp[3:4, :] + (cp[0:1, :] * g2 + cp[1:2, :] * g1 + cp[2:3, :] * gate)
        act = conv * (1.0 / (1.0 + jnp.exp(-conv))) * val
        o_ref[rows, :] = act.astype(o_ref.dtype)
        prev_rows = gate[mc - halo:mc, :]
    tail_ref[j] = prev_rows


def ffn_up(xn, w_up, w_conv, b_conv, seq, *, tm_target=2048, tn_target=256, chunk_rows=512):
    m, d = xn.shape
    f = w_up.shape[1] // 2
    tm = _pick_tile(seq, tm_target, 2 * V7X_SUBLANES)
    tn = _pick_tile(f, tn_target, V7X_LANES)
    nj = f // tn
    halo = V7X_SUBLANES
    row_splits = tm // chunk_rows if tm % chunk_rows == 0 else 1
    assert tm // row_splits >= halo and CONV_WIDTH - 1 <= halo and CONV_WIDTH + 1 <= V7X_SUBLANES
    conv_params = jnp.concatenate(
        [w_conv, b_conv.reshape(1, f),
         jnp.zeros((V7X_SUBLANES - CONV_WIDTH - 1, f), F32)], axis=0)
    conv_params = conv_params.reshape(V7X_SUBLANES, nj, tn).transpose(1, 0, 2)
    return pl.pallas_call(
        functools.partial(_ffn_up_kernel, tiles_per_seq=seq // tm, row_splits=row_splits),
        grid=(m // tm, nj),
        in_specs=[
            pl.BlockSpec((tm, d), lambda i, j: (i, 0)),
            pl.BlockSpec((d, tn), lambda i, j: (0, j)),
            pl.BlockSpec((d, tn), lambda i, j: (0, j + nj)),
            pl.BlockSpec((nj, V7X_SUBLANES, tn), lambda i, j: (0, 0, 0)),
        ],
        out_specs=pl.BlockSpec((tm, tn), lambda i, j: (i, j)),
        out_shape=jax.ShapeDtypeStruct((m, f), BF16),
        scratch_shapes=[
            pltpu.VMEM((halo, tn), F32),
            pltpu.VMEM((nj, halo, tn), F32),
        ],
        compiler_params=_params("arbitrary", "arbitrary"),
        name="ffn_up",
    )(xn, w_up, w_up, conv_params)


def _sb_attn_kernel(q_ref, k_ref, v_ref, u2_ref, o_ref,
                    acc_ref, run_ref, z_ref, hl_ref, incl_ref, w_ref, *, tq, tk, heads, scale):
    seq = q_ref.shape[0]
    nsub = tq // tk
    reps = tk // V7X_LANES
    score_scale = scale * LOG2_E
    nchunk = tq // ATTN_ROW_CHUNK

    def strict_mask(r):
        shape = (ATTN_ROW_CHUNK, tq)
        return (lax.broadcasted_iota(jnp.int32, shape, 1)
                < r * ATTN_ROW_CHUNK + lax.broadcasted_iota(jnp.int32, shape, 0))

    def one_block(q0, kb, masked):
        k0 = pl.multiple_of(kb * tq, tq)
        for hh in range(heads):
            cols_h = slice(hh * HEAD_DIM, (hh + 1) * HEAD_DIM)
            z_ref[hh] = lax.dot_general(q_ref[pl.ds(q0, tq), cols_h], k_ref[pl.ds(k0, tq), cols_h],
                                        (((1,), (1,)), ((), ())), preferred_element_type=F32)
            for r in range(nchunk):
                rows = slice(r * ATTN_ROW_CHUNK, (r + 1) * ATTN_ROW_CHUNK)
                nz2 = z_ref[hh, rows, :] * (-score_scale)
                z_ref[hh, rows, :] = nz2
                neg_abs = lax.bitcast_convert_type(
                    lax.bitcast_convert_type(nz2, jnp.uint32) | jnp.uint32(F32_SIGN_BIT), F32)
                lsm = jnp.minimum(nz2, 0.0) - jnp.log2(1.0 + jnp.exp2(neg_abs))
                if masked:
                    lsm = jnp.where(strict_mask(r), lsm, 0.0)
                for c in range(nsub):
                    sub = lsm[:, c * tk:(c + 1) * tk]
                    hi = sub.astype(BF16)
                    hl_ref[hh, c, rows, 0:tk] = hi
                    hl_ref[hh, c, rows, tk:2 * tk] = (sub - hi.astype(F32)).astype(BF16)
            for c in range(nsub):
                incl_ref[hh, c] = jnp.dot(hl_ref[hh, c], u2_ref[...], preferred_element_type=F32)
            for r in range(nchunk):
                rows = slice(r * ATTN_ROW_CHUNK, (r + 1) * ATTN_ROW_CHUNK)
                run = run_ref[hh, rows, :]
                for c in reversed(range(nsub)):
                    cols = slice(c * tk, (c + 1) * tk)
                    incl = incl_ref[hh, c, rows, :]
                    e2 = incl + jnp.tile(run, (1, reps)) - z_ref[hh, rows, cols]
                    w = jnp.exp2(e2)
                    if masked:
                        w = jnp.where(strict_mask(r)[:, cols], w, 0.0)
                    w_ref[hh, rows, cols] = w.astype(BF16)
                    run = run + jnp.broadcast_to(incl[:, 0:1], run.shape)
                run_ref[hh, rows, :] = run
            acc_ref[hh] += jnp.dot(w_ref[hh], v_ref[pl.ds(k0, tq), cols_h],
                                   preferred_element_type=F32)

    def q_body(qi, carry):
        q0 = pl.multiple_of(qi * tq, tq)
        acc_ref[...] = jnp.zeros_like(acc_ref)
        run_ref[...] = jnp.zeros_like(run_ref)
        one_block(q0, qi, True)

        def kb_body(n, c):
            one_block(q0, qi - 1 - n, False)
            return c

        lax.fori_loop(0, qi, kb_body, 0)
        for hh in range(heads):
            o_ref[pl.ds(q0, tq), hh * HEAD_DIM:(hh + 1) * HEAD_DIM] = acc_ref[hh].astype(o_ref.dtype)
        return carry

    lax.fori_loop(0, seq // tq, q_body, 0)


def stick_breaking_attention(qkv, n_heads, *, tq_target=512, tk=V7X_MXU_DIM, heads_per_step=4):
    b, s, _ = qkv.shape
    tk = min(tk, s)
    tq = _pick_tile(s, tq_target, tk)
    hps = heads_per_step if n_heads % heads_per_step == 0 else 1
    nstep = n_heads // hps
    width = hps * HEAD_DIM
    u = (lax.broadcasted_iota(jnp.int32, (tk, tk), 0)
         >= lax.broadcasted_iota(jnp.int32, (tk, tk), 1)).astype(BF16)
    u2 = jnp.concatenate([u, u], axis=0)
    kern = functools.partial(_sb_attn_kernel, tq=tq, tk=tk, heads=hps, scale=HEAD_DIM ** -0.5)
    return pl.pallas_call(
        kern,
        grid=(b, nstep),
        in_specs=[
            pl.BlockSpec((None, s, width), lambda bi, h: (bi, 0, h)),
            pl.BlockSpec((None, s, width), lambda bi, h: (bi, 0, nstep + h)),
            pl.BlockSpec((None, s, width), lambda bi, h: (bi, 0, 2 * nstep + h)),
            pl.BlockSpec((2 * tk, tk), lambda bi, h: (0, 0)),
        ],
        out_specs=pl.BlockSpec((None, s, width), lambda bi, h: (bi, 0, h)),
        out_shape=jax.ShapeDtypeStruct((b, s, n_heads * HEAD_DIM), BF16),
        scratch_shapes=[
            pltpu.VMEM((hps, tq, HEAD_DIM), F32),
            pltpu.VMEM((hps, tq, V7X_LANES), F32),
            pltpu.VMEM((hps, tq, tq), F32),
            pltpu.VMEM((hps, tq // tk, tq, 2 * tk), BF16),
            pltpu.VMEM((hps, tq // tk, tq, tk), F32),
            pltpu.VMEM((hps, tq, tq), BF16),
        ],
        compiler_params=_params("parallel", "parallel"),
        name="stick_breaking_attention",
    )(qkv, qkv, qkv, u2)


def _fox_attn_kernel(q_ref, k_ref, v_ref, ccol_ref, crow_ref, o_ref,
                     acc_ref, m_ref, l_ref, bt_ref, *, tq, group, scale):
    seq = k_ref.shape[0]
    kvh = pl.program_id(1)
    reps = tq // V7X_LANES
    score_scale = scale * LOG2_E

    def one_block(q0, kb, masked):
        k0 = pl.multiple_of(kb * tq, tq)
        kblk = k_ref[pl.ds(k0, tq), :]
        vblk = v_ref[pl.ds(k0, tq), :]
        for g in range(group):
            heads = slice(g * HEAD_DIM, (g + 1) * HEAD_DIM)
            s_val = lax.dot_general(q_ref[pl.ds(q0, tq), heads], kblk,
                                    (((1,), (1,)), ((), ())), preferred_element_type=F32)
            cs = crow_ref[g:g + 1, pl.ds(k0, tq)] * LOG2_E
            p_parts = []
            for r in range(tq // ATTN_ROW_CHUNK):
                rows = slice(r * ATTN_ROW_CHUNK, (r + 1) * ATTN_ROW_CHUNK)
                grows = slice(g * tq + rows.start, g * tq + rows.stop)
                u = s_val[rows, :] * score_scale - cs
                if masked:
                    shape = (ATTN_ROW_CHUNK, tq)
                    causal = (lax.broadcasted_iota(jnp.int32, shape, 1)
                              <= rows.start + lax.broadcasted_iota(jnp.int32, shape, 0))
                    u = jnp.where(causal, u, MASK_VALUE)
                bt = bt_ref[grows, :]
                m_old = m_ref[grows, :]
                m_new = jnp.maximum(m_old, jnp.max(u, axis=-1, keepdims=True) + bt)
                alpha = jnp.exp2(m_old - m_new)
                p = jnp.exp2(u - jnp.tile(m_new - bt, (1, reps)))
                l_ref[grows, :] = alpha * l_ref[grows, :] + jnp.sum(p, axis=-1, keepdims=True)
                m_ref[grows, :] = m_new
                acc_ref[grows, :] = alpha * acc_ref[grows, :]
                p_parts.append(p.astype(BF16))
            acc_ref[g * tq:(g + 1) * tq, :] += jnp.dot(jnp.concatenate(p_parts, axis=0), vblk,
                                                       preferred_element_type=F32)

    def q_body(qi, carry):
        q0 = pl.multiple_of(qi * tq, tq)
        cblk = ccol_ref[pl.ds(q0, tq), :]
        lane = lax.broadcasted_iota(jnp.int32, cblk.shape, 1)
        for g in range(group):
            c_t = jnp.sum(jnp.where(lane == kvh * group + g, cblk, 0.0), axis=-1, keepdims=True)
            bt_ref[g * tq:(g + 1) * tq, :] = jnp.broadcast_to(c_t * LOG2_E, (tq, V7X_LANES))
        acc_ref[...] = jnp.zeros_like(acc_ref)
        l_ref[...] = jnp.zeros_like(l_ref)
        m_ref[...] = jnp.full_like(m_ref, MASK_VALUE)
        one_block(q0, qi, True)

        def kb_body(kb, c):
            one_block(q0, kb, False)
            return c

        lax.fori_loop(0, qi, kb_body, 0)
        out = acc_ref[...] / l_ref[...]
        for g in range(group):
            o_ref[pl.ds(q0, tq), g * HEAD_DIM:(g + 1) * HEAD_DIM] = (
                out[g * tq:(g + 1) * tq, :].astype(o_ref.dtype))
        return carry

    lax.fori_loop(0, seq // tq, q_body, 0)


def forgetting_attention(q, kv, c_col, c_row, n_kv_heads, *, tq_target=512):
    b, s, dq = q.shape
    assert HEAD_DIM == V7X_LANES
    group = dq // HEAD_DIM // n_kv_heads
    tq = _pick_tile(s, tq_target, V7X_LANES)
    cw = c_col.shape[2]
    kern = functools.partial(_fox_attn_kernel, tq=tq, group=group, scale=HEAD_DIM ** -0.5)
    stat = pltpu.VMEM((group * tq, V7X_LANES), F32)
    return pl.pallas_call(
        kern,
        grid=(b, n_kv_heads),
        in_specs=[
            pl.BlockSpec((None, s, group * HEAD_DIM), lambda bi, h: (bi, 0, h)),
            pl.BlockSpec((None, s, HEAD_DIM), lambda bi, h: (bi, 0, h)),
            pl.BlockSpec((None, s, HEAD_DIM), lambda bi, h: (bi, 0, n_kv_heads + h)),
            pl.BlockSpec((None, s, cw), lambda bi, h: (bi, 0, 0)),
            pl.BlockSpec((None, None, group, s), lambda bi, h: (bi, h, 0, 0)),
        ],
        out_specs=pl.BlockSpec((None, s, group * HEAD_DIM), lambda bi, h: (bi, 0, h)),
        out_shape=jax.ShapeDtypeStruct((b, s, dq), BF16),
        scratch_shapes=[stat, stat, stat, stat],
        compiler_params=_params("parallel", "parallel"),
        name="forgetting_attention",
    )(q, kv, kv, c_col, c_row)


def _rms_norm_kernel(h_ref, g_ref, o_ref):
    x = h_ref[...]
    ms = jnp.mean(x * x, axis=-1, keepdims=True)
    o_ref[...] = x * lax.rsqrt(ms + RMS_EPS) * g_ref[...]


def rms_norm(h, gain, *, tm_target=256):
    m, d = h.shape
    tm = _pick_tile(m, tm_target, V7X_SUBLANES)
    return pl.pallas_call(
        _rms_norm_kernel,
        grid=(m // tm,),
        in_specs=[pl.BlockSpec((tm, d), lambda i: (i, 0)),
                  pl.BlockSpec((1, d), lambda i: (0, 0))],
        out_specs=pl.BlockSpec((tm, d), lambda i: (i, 0)),
        out_shape=jax.ShapeDtypeStruct((m, d), F32),
        compiler_params=_params("parallel"),
        name="final_rms_norm",
    )(h, gain.reshape(1, d))


def _conv_glu_ffn(h, gain, w_up, w_conv, b_conv, w_down, seq):
    (xn,) = rms_norm_cast(h, [gain])
    act = ffn_up(xn, w_up.astype(BF16), w_conv, b_conv, seq)
    return matmul_residual(act, w_down.astype(BF16), h, tm_target=512, tn_target=512)


def kernel(x, a_attn_norm, a_w_qkv, a_w_o, a_ffn_norm, a_w_up, a_w_conv, a_b_conv, a_w_down,
           kv_norm, w_kv, w_f, b_f,
           b_attn_norm, b_w_q, b_w_o, b_ffn_norm, b_w_up, b_w_conv, b_b_conv, b_w_down,
           final_norm):
    bsz, seq, d = x.shape
    m = bsz * seq
    n_heads = d // HEAD_DIM
    n_kv_heads = w_kv.shape[1] // (2 * HEAD_DIM)
    group = n_heads // n_kv_heads
    n_gates = w_f.shape[1]
    assert n_gates == n_heads and n_gates <= V7X_LANES

    h = x.reshape(m, d)

    for i in range(a_w_qkv.shape[0]):
        (xn,) = rms_norm_cast(h, [a_attn_norm[i]])
        qkv = matmul(xn, a_w_qkv[i].astype(BF16))
        o = stick_breaking_attention(qkv.reshape(bsz, seq, 3 * d), n_heads)
        h = matmul_residual(o.reshape(m, d), a_w_o[i].astype(BF16), h, tm_target=1024, tn_target=1024)
        h = _conv_glu_ffn(h, a_ffn_norm[i], a_w_up[i], a_w_conv[i], a_b_conv[i], a_w_down[i], seq)

    kv = c_col = c_row = None
    for j in range(b_w_q.shape[0]):
        if j == 0:
            xn_kv, xn = rms_norm_cast(h, [kv_norm, b_attn_norm[j]])
            wf_pad = jnp.pad(w_f, ((0, 0), (0, V7X_LANES - n_gates))).astype(BF16)
            bf_pad = jnp.pad(b_f, (0, V7X_LANES - n_gates)).reshape(1, V7X_LANES)
            kv, log_f = kv_proj(xn_kv, w_kv.astype(BF16), wf_pad, bf_pad)
            kv = kv.reshape(bsz, seq, 2 * n_kv_heads * HEAD_DIM)
            c_col = cumsum_positions(log_f.reshape(bsz, seq, V7X_LANES))
            c_row = (c_col[:, :, :n_gates].transpose(0, 2, 1)
                     .reshape(bsz, n_kv_heads, group, seq))
        else:
            (xn,) = rms_norm_cast(h, [b_attn_norm[j]])
        q = matmul(xn, b_w_q[j].astype(BF16))
        o = forgetting_attention(q.reshape(bsz, seq, d), kv, c_col, c_row, n_kv_heads)
        h = matmul_residual(o.reshape(m, d), b_w_o[j].astype(BF16), h, tm_target=1024, tn_target=1024)
        h = _conv_glu_ffn(h, b_ffn_norm[j], b_w_up[j], b_w_conv[j], b_b_conv[j], b_w_down[j], seq)

    return rms_norm(h, final_norm).reshape(bsz, seq, d)
```

```python
import functools

import jax
import jax.numpy as jnp
from jax import lax
from jax.experimental import pallas as pl
from jax.experimental.pallas import tpu as pltpu

HEAD_DIM = 128
CONV_WIDTH = 3
RMS_EPS = 1e-6

V7X_LANES = 128
V7X_SUBLANES = 8
V7X_MXU_DIM = 256
V7X_VMEM_LIMIT_BYTES = 56 * 1024 * 1024

F32 = jnp.float32
BF16 = jnp.bfloat16

NORM_ROW_CHUNK = 32
ATTN_ROW_CHUNK = 32
MASK_VALUE = -1e30
LOG2_E = 1.4426950408889634
F32_SIGN_BIT = 0x80000000
FOX_Q_SCALE = HEAD_DIM ** -0.5 * LOG2_E
SB_Q_SCALE = -FOX_Q_SCALE


def _params(*semantics):
    return pltpu.CompilerParams(dimension_semantics=semantics,
                                vmem_limit_bytes=V7X_VMEM_LIMIT_BYTES)


def _pick_tile(n, target, quantum):
    if n <= target:
        return n
    t = (target // quantum) * quantum
    while t >= quantum:
        if n % t == 0:
            return t
        t -= quantum
    raise ValueError(f"no tile for {n}")


def _rms_norm_cast_kernel(h_ref, g_ref, *xn_refs):
    tm = h_ref.shape[0]

    def body(c, carry):
        r0 = pl.multiple_of(c * NORM_ROW_CHUNK, NORM_ROW_CHUNK)
        x = h_ref[pl.ds(r0, NORM_ROW_CHUNK), :]
        ms = jnp.mean(x * x, axis=-1, keepdims=True)
        y = x * lax.rsqrt(ms + RMS_EPS)
        for k, xn_ref in enumerate(xn_refs):
            xn_ref[pl.ds(r0, NORM_ROW_CHUNK), :] = (y * g_ref[k:k + 1, :]).astype(xn_ref.dtype)
        return carry

    lax.fori_loop(0, tm // NORM_ROW_CHUNK, body, 0, unroll=2)


def rms_norm_cast(h, gains, *, tm_target=512):
    m, d = h.shape
    ng = len(gains)
    tm = _pick_tile(m, tm_target, 2 * NORM_ROW_CHUNK)
    out = pl.pallas_call(
        _rms_norm_cast_kernel,
        grid=(m // tm,),
        in_specs=[pl.BlockSpec((tm, d), lambda i: (i, 0)),
                  pl.BlockSpec((ng, d), lambda i: (0, 0))],
        out_specs=[pl.BlockSpec((tm, d), lambda i: (i, 0))] * ng,
        out_shape=[jax.ShapeDtypeStruct((m, d), BF16)] * ng,
        compiler_params=_params("parallel"),
        name="rms_norm_cast",
    )(h, jnp.stack(gains))
    return list(out)


def _matmul_kernel(x_ref, w_ref, o_ref, *, scaled_tiles, scale):
    acc = jnp.dot(x_ref[...], w_ref[...], preferred_element_type=F32)
    if scaled_tiles:
        acc = acc * jnp.where(pl.program_id(1) < scaled_tiles, scale, 1.0)
    o_ref[...] = acc.astype(o_ref.dtype)


def matmul(x, w, *, scaled_cols=0, scale=1.0, tm_target=2048, tn_target=512):
    m, k = x.shape
    n = w.shape[1]
    tm = _pick_tile(m, tm_target, 2 * V7X_SUBLANES)
    tn = _pick_tile(n, tn_target, V7X_LANES)
    assert scaled_cols % tn == 0
    return pl.pallas_call(
        functools.partial(_matmul_kernel, scaled_tiles=scaled_cols // tn, scale=scale),
        grid=(m // tm, n // tn),
        in_specs=[
            pl.BlockSpec((tm, k), lambda i, j: (i, 0)),
            pl.BlockSpec((k, tn), lambda i, j: (0, j)),
        ],
        out_specs=pl.BlockSpec((tm, tn), lambda i, j: (i, j)),
        out_shape=jax.ShapeDtypeStruct((m, n), BF16),
        compiler_params=_params("parallel", "parallel"),
        name="matmul",
    )(x, w)


def _kv_proj_kernel(x_ref, w_ref, wf_ref, bf_ref, o_ref, lf_ref):
    @pl.when(pl.program_id(1) == 0)
    def _():
        pre = jnp.dot(x_ref[...], wf_ref[...], preferred_element_type=F32) + bf_ref[...]
        lf_ref[...] = -(jnp.maximum(-pre, 0.0) + jnp.log(1.0 + jnp.exp(-jnp.abs(pre))))

    o_ref[...] = jnp.dot(x_ref[...], w_ref[...], preferred_element_type=F32).astype(o_ref.dtype)


def kv_proj(x, w_kv, wf_pad, bf_pad, *, tm_target=1024, tn_target=1024):
    m, d = x.shape
    n = w_kv.shape[1]
    nf = wf_pad.shape[1]
    tm = _pick_tile(m, tm_target, 2 * V7X_SUBLANES)
    tn = _pick_tile(n, tn_target, V7X_LANES)
    return pl.pallas_call(
        _kv_proj_kernel,
        grid=(m // tm, n // tn),
        in_specs=[
            pl.BlockSpec((tm, d), lambda i, j: (i, 0)),
            pl.BlockSpec((d, tn), lambda i, j: (0, j)),
            pl.BlockSpec((d, nf), lambda i, j: (0, 0)),
            pl.BlockSpec((1, nf), lambda i, j: (0, 0)),
        ],
        out_specs=[
            pl.BlockSpec((tm, tn), lambda i, j: (i, j)),
            pl.BlockSpec((tm, nf), lambda i, j: (i, 0)),
        ],
        out_shape=[
            jax.ShapeDtypeStruct((m, n), BF16),
            jax.ShapeDtypeStruct((m, nf), F32),
        ],
        compiler_params=_params("parallel", "arbitrary"),
        name="kv_proj",
    )(x, w_kv, wf_pad, bf_pad)


def _split3(x):
    hi = x.astype(BF16)
    r1 = x - hi.astype(F32)
    mid = r1.astype(BF16)
    lo = (r1 - mid.astype(F32)).astype(BF16)
    return hi, mid, lo


def _cumsum_kernel(x_ref, tri_ref, c_ref, run_ref):
    blk = tri_ref.shape[0]
    nblk = x_ref.shape[0] // blk
    run_ref[...] = jnp.zeros_like(run_ref)

    def body(n, carry):
        r0 = pl.multiple_of(n * blk, blk)
        hi, mid, lo = _split3(x_ref[pl.ds(r0, blk), :])
        tri = tri_ref[...]
        incl = (jnp.dot(tri, hi, preferred_element_type=F32)
                + jnp.dot(tri, mid, preferred_element_type=F32)
                + jnp.dot(tri, lo, preferred_element_type=F32)
                + run_ref[0:1, :])
        c_ref[pl.ds(r0, blk), :] = incl
        run_ref[0:1, :] = incl[blk - 1:blk, :]
        return carry

    lax.fori_loop(0, nblk, body, 0)


def cumsum_positions(x):
    b, s, c = x.shape
    blk = V7X_LANES
    tri = (lax.broadcasted_iota(jnp.int32, (blk, blk), 1)
           <= lax.broadcasted_iota(jnp.int32, (blk, blk), 0)).astype(BF16)
    return pl.pallas_call(
        _cumsum_kernel,
        grid=(b,),
        in_specs=[
            pl.BlockSpec((None, s, c), lambda i: (i, 0, 0)),
            pl.BlockSpec((blk, blk), lambda i: (0, 0)),
        ],
        out_specs=pl.BlockSpec((None, s, c), lambda i: (i, 0, 0)),
        out_shape=jax.ShapeDtypeStruct((b, s, c), F32),
        scratch_shapes=[pltpu.VMEM((V7X_SUBLANES, c), F32)],
        compiler_params=_params("parallel"),
        name="cumsum_positions",
    )(x, tri)


def _matmul_residual_kernel(a_ref, w_ref, r_ref, o_ref):
    o_ref[...] = r_ref[...] + jnp.dot(a_ref[...], w_ref[...], preferred_element_type=F32)


def matmul_residual(a, w, res, *, tm_target, tn_target):
    m, k = a.shape
    n = w.shape[1]
    tm = _pick_tile(m, tm_target, 2 * V7X_SUBLANES)
    tn = _pick_tile(n, tn_target, V7X_LANES)
    return pl.pallas_call(
        _matmul_residual_kernel,
        grid=(m // tm, n // tn),
        in_specs=[
            pl.BlockSpec((tm, k), lambda i, j: (i, 0)),
            pl.BlockSpec((k, tn), lambda i, j: (0, j)),
            pl.BlockSpec((tm, tn), lambda i, j: (i, j)),
        ],
        out_specs=pl.BlockSpec((tm, tn), lambda i, j: (i, j)),
        out_shape=jax.ShapeDtypeStruct((m, n), F32),
        compiler_params=_params("parallel", "parallel"),
        name="matmul_residual",
    )(a, w, res)


def _ffn_up_kernel(xn_ref, wg_ref, wv_ref, cp_ref, o_ref, halo_ref, tail_ref,
                   *, tiles_per_seq, row_splits):
    i = pl.program_id(0)
    j = pl.program_id(1)
    tm, tn = o_ref.shape
    halo = V7X_SUBLANES
    mc = tm // row_splits

    seq_start = (i % tiles_per_seq) == 0

    @pl.when(seq_start)
    def _():
        halo_ref[...] = jnp.zeros_like(halo_ref)

    @pl.when(jnp.logical_not(seq_start))
    def _():
        halo_ref[...] = tail_ref[j]

    cp = cp_ref[j]
    prev_rows = halo_ref[...]
    for c in range(row_splits):
        rows = slice(c * mc, (c + 1) * mc)
        xc = xn_ref[rows, :]
        gate = jnp.dot(xc, wg_ref[...], preferred_element_type=F32)
        val = jnp.dot(xc, wv_ref[...], preferred_element_type=F32)
        ext = jnp.concatenate([prev_rows, gate], axis=0)
        g1 = pltpu.roll(ext, 1, 0)[halo:, :]
        g2 = pltpu.roll(ext, 2, 0)[halo:, :]
        conv = cp[3:4, :] + (cp[0:1, :] * g2 + cp[1:2, :] * g1 + cp[2:3, :] * gate)
        act = conv * (1.0 / (1.0 + jnp.exp(-conv))) * val
        o_ref[rows, :] = act.astype(o_ref.dtype)
        prev_rows = gate[mc - halo:mc, :]
    tail_ref[j] = prev_rows


def ffn_up(xn, w_up, w_conv, b_conv, seq, *, tm_target=2048, tn_target=256, chunk_rows=512):
    m, d = xn.shape
    f = w_up.shape[1] // 2
    tm = _pick_tile(seq, tm_target, 2 * V7X_SUBLANES)
    tn = _pick_tile(f, tn_target, V7X_LANES)
    nj = f // tn
    halo = V7X_SUBLANES
    row_splits = tm // chunk_rows if tm % chunk_rows == 0 else 1
    assert tm // row_splits >= halo and CONV_WIDTH - 1 <= halo and CONV_WIDTH + 1 <= V7X_SUBLANES
    conv_params = jnp.concatenate(
        [w_conv, b_conv.reshape(1, f),
         jnp.zeros((V7X_SUBLANES - CONV_WIDTH - 1, f), F32)], axis=0)
    conv_params = conv_params.reshape(V7X_SUBLANES, nj, tn).transpose(1, 0, 2)
    return pl.pallas_call(
        functools.partial(_ffn_up_kernel, tiles_per_seq=seq // tm, row_splits=row_splits),
        grid=(m // tm, nj),
        in_specs=[
            pl.BlockSpec((tm, d), lambda i, j: (i, 0)),
            pl.BlockSpec((d, tn), lambda i, j: (0, j)),
            pl.BlockSpec((d, tn), lambda i, j: (0, j + nj)),
            pl.BlockSpec((nj, V7X_SUBLANES, tn), lambda i, j: (0, 0, 0)),
        ],
        out_specs=pl.BlockSpec((tm, tn), lambda i, j: (i, j)),
        out_shape=jax.ShapeDtypeStruct((m, f), BF16),
        scratch_shapes=[
            pltpu.VMEM((halo, tn), F32),
            pltpu.VMEM((nj, halo, tn), F32),
        ],
        compiler_params=_params("arbitrary", "arbitrary"),
        name="ffn_up",
    )(xn, w_up, w_up, conv_params)


def _sb_attn_kernel(q_ref, k_ref, v_ref, u_ref, o_ref,
                    acc_ref, run_ref, z_ref, lsm_ref, rsum_ref, incl_ref, w_ref, *, tq, tk, heads):
    seq = q_ref.shape[0]
    nsub = tq // tk
    reps = tk // V7X_LANES
    nchunk = tq // ATTN_ROW_CHUNK

    def strict_mask(r):
        shape = (ATTN_ROW_CHUNK, tq)
        return (lax.broadcasted_iota(jnp.int32, shape, 1)
                < r * ATTN_ROW_CHUNK + lax.broadcasted_iota(jnp.int32, shape, 0))

    def one_block(q0, kb, masked):
        k0 = pl.multiple_of(kb * tq, tq)
        for hh in range(heads):
            cols_h = slice(hh * HEAD_DIM, (hh + 1) * HEAD_DIM)
            z_ref[hh] = lax.dot_general(q_ref[pl.ds(q0, tq), cols_h], k_ref[pl.ds(k0, tq), cols_h],
                                        (((1,), (1,)), ((), ())), preferred_element_type=F32)
            for r in range(nchunk):
                rows = slice(r * ATTN_ROW_CHUNK, (r + 1) * ATTN_ROW_CHUNK)
                nz2 = z_ref[hh, rows, :]
                neg_abs = lax.bitcast_convert_type(
                    lax.bitcast_convert_type(nz2, jnp.uint32) | jnp.uint32(F32_SIGN_BIT), F32)
                lsm = jnp.minimum(nz2, 0.0) - jnp.log2(1.0 + jnp.exp2(neg_abs))
                if masked:
                    lsm = jnp.where(strict_mask(r), lsm, 0.0)
                lsm_ref[hh, rows, :] = lsm.astype(BF16)
                for c in range(nsub):
                    total = jnp.sum(lsm[:, c * tk:(c + 1) * tk], axis=-1, keepdims=True)
                    rsum_ref[hh, c, rows, :] = jnp.broadcast_to(total, (ATTN_ROW_CHUNK, V7X_LANES))
            for c in range(nsub):
                incl_ref[hh, c] = jnp.dot(lsm_ref[hh, :, c * tk:(c + 1) * tk], u_ref[...],
                                          preferred_element_type=F32)
            for r in range(nchunk):
                rows = slice(r * ATTN_ROW_CHUNK, (r + 1) * ATTN_ROW_CHUNK)
                run = run_ref[hh, rows, :]
                for c in reversed(range(nsub)):
                    cols = slice(c * tk, (c + 1) * tk)
                    e2 = incl_ref[hh, c, rows, :] + jnp.tile(run, (1, reps)) - z_ref[hh, rows, cols]
                    w = jnp.exp2(e2)
                    if masked:
                        w = jnp.where(strict_mask(r)[:, cols], w, 0.0)
                    w_ref[hh, rows, cols] = w.astype(BF16)
                    run = run + rsum_ref[hh, c, rows, :]
                run_ref[hh, rows, :] = run
            acc_ref[hh] += jnp.dot(w_ref[hh], v_ref[pl.ds(k0, tq), cols_h],
                                   preferred_element_type=F32)

    def q_body(qi, carry):
        q0 = pl.multiple_of(qi * tq, tq)
        acc_ref[...] = jnp.zeros_like(acc_ref)
        run_ref[...] = jnp.zeros_like(run_ref)
        one_block(q0, qi, True)

        def kb_body(n, c):
            one_block(q0, qi - 1 - n, False)
            return c

        lax.fori_loop(0, qi, kb_body, 0)
        for hh in range(heads):
            o_ref[pl.ds(q0, tq), hh * HEAD_DIM:(hh + 1) * HEAD_DIM] = acc_ref[hh].astype(o_ref.dtype)
        return carry

    lax.fori_loop(0, seq // tq, q_body, 0)


def stick_breaking_attention(qkv, n_heads, *, tq_target=512, tk=V7X_MXU_DIM, heads_per_step=4):
    b, s, _ = qkv.shape
    tk = min(tk, s)
    tq = _pick_tile(s, tq_target, tk)
    hps = heads_per_step if n_heads % heads_per_step == 0 else 1
    nstep = n_heads // hps
    width = hps * HEAD_DIM
    u = (lax.broadcasted_iota(jnp.int32, (tk, tk), 0)
         >= lax.broadcasted_iota(jnp.int32, (tk, tk), 1)).astype(BF16)
    kern = functools.partial(_sb_attn_kernel, tq=tq, tk=tk, heads=hps)
    return pl.pallas_call(
        kern,
        grid=(b, nstep),
        in_specs=[
            pl.BlockSpec((None, s, width), lambda bi, h: (bi, 0, h)),
            pl.BlockSpec((None, s, width), lambda bi, h: (bi, 0, nstep + h)),
            pl.BlockSpec((None, s, width), lambda bi, h: (bi, 0, 2 * nstep + h)),
            pl.BlockSpec((tk, tk), lambda bi, h: (0, 0)),
        ],
        out_specs=pl.BlockSpec((None, s, width), lambda bi, h: (bi, 0, h)),
        out_shape=jax.ShapeDtypeStruct((b, s, n_heads * HEAD_DIM), BF16),
        scratch_shapes=[
            pltpu.VMEM((hps, tq, HEAD_DIM), F32),
            pltpu.VMEM((hps, tq, V7X_LANES), F32),
            pltpu.VMEM((hps, tq, tq), F32),
            pltpu.VMEM((hps, tq, tq), BF16),
            pltpu.VMEM((hps, tq // tk, tq, V7X_LANES), F32),
            pltpu.VMEM((hps, tq // tk, tq, tk), F32),
            pltpu.VMEM((hps, tq, tq), BF16),
        ],
        compiler_params=_params("parallel", "parallel"),
        name="stick_breaking_attention",
    )(qkv, qkv, qkv, u)


def _fox_attn_kernel(q_ref, k_ref, v_ref, ccol_ref, crow_ref, o_ref,
                     acc_ref, m_ref, l_ref, bt_ref, *, tq, group):
    seq = k_ref.shape[0]
    kvh = pl.program_id(1)
    reps = tq // V7X_LANES

    def one_block(q0, kb, masked):
        k0 = pl.multiple_of(kb * tq, tq)
        kblk = k_ref[pl.ds(k0, tq), :]
        vblk = v_ref[pl.ds(k0, tq), :]
        for g in range(group):
            heads = slice(g * HEAD_DIM, (g + 1) * HEAD_DIM)
            s_val = lax.dot_general(q_ref[pl.ds(q0, tq), heads], kblk,
                                    (((1,), (1,)), ((), ())), preferred_element_type=F32)
            cs = crow_ref[g:g + 1, pl.ds(k0, tq)] * LOG2_E
            p_parts = []
            for r in range(tq // ATTN_ROW_CHUNK):
                rows = slice(r * ATTN_ROW_CHUNK, (r + 1) * ATTN_ROW_CHUNK)
                grows = slice(g * tq + rows.start, g * tq + rows.stop)
                u = s_val[rows, :] - cs
                if masked:
                    shape = (ATTN_ROW_CHUNK, tq)
                    causal = (lax.broadcasted_iota(jnp.int32, shape, 1)
                              <= rows.start + lax.broadcasted_iota(jnp.int32, shape, 0))
                    u = jnp.where(causal, u, MASK_VALUE)
                bt = bt_ref[grows, :]
                m_old = m_ref[grows, :]
                m_new = jnp.maximum(m_old, jnp.max(u, axis=-1, keepdims=True) + bt)
                alpha = jnp.exp2(m_old - m_new)
                p = jnp.exp2(u - jnp.tile(m_new - bt, (1, reps)))
                l_ref[grows, :] = alpha * l_ref[grows, :] + jnp.sum(p, axis=-1, keepdims=True)
                m_ref[grows, :] = m_new
                acc_ref[grows, :] = alpha * acc_ref[grows, :]
                p_parts.append(p.astype(BF16))
            acc_ref[g * tq:(g + 1) * tq, :] += jnp.dot(jnp.concatenate(p_parts, axis=0), vblk,
                                                       preferred_element_type=F32)

    def q_body(qi, carry):
        q0 = pl.multiple_of(qi * tq, tq)
        cblk = ccol_ref[pl.ds(q0, tq), :]
        lane = lax.broadcasted_iota(jnp.int32, cblk.shape, 1)
        for g in range(group):
            c_t = jnp.sum(jnp.where(lane == kvh * group + g, cblk, 0.0), axis=-1, keepdims=True)
            bt_ref[g * tq:(g + 1) * tq, :] = jnp.broadcast_to(c_t * LOG2_E, (tq, V7X_LANES))
        acc_ref[...] = jnp.zeros_like(acc_ref)
        l_ref[...] = jnp.zeros_like(l_ref)
        m_ref[...] = jnp.full_like(m_ref, MASK_VALUE)
        one_block(q0, qi, True)

        def kb_body(kb, c):
            one_block(q0, kb, False)
            return c

        lax.fori_loop(0, qi, kb_body, 0)
        out = acc_ref[...] / l_ref[...]
        for g in range(group):
            o_ref[pl.ds(q0, tq), g * HEAD_DIM:(g + 1) * HEAD_DIM] = (
                out[g * tq:(g + 1) * tq, :].astype(o_ref.dtype))
        return carry

    lax.fori_loop(0, seq // tq, q_body, 0)


def forgetting_attention(q, kv, c_col, c_row, n_kv_heads, *, tq_target=512):
    b, s, dq = q.shape
    assert HEAD_DIM == V7X_LANES
    group = dq // HEAD_DIM // n_kv_heads
    tq = _pick_tile(s, tq_target, V7X_LANES)
    cw = c_col.shape[2]
    kern = functools.partial(_fox_attn_kernel, tq=tq, group=group)
    stat = pltpu.VMEM((group * tq, V7X_LANES), F32)
    return pl.pallas_call(
        kern,
        grid=(b, n_kv_heads),
        in_specs=[
            pl.BlockSpec((None, s, group * HEAD_DIM), lambda bi, h: (bi, 0, h)),
            pl.BlockSpec((None, s, HEAD_DIM), lambda bi, h: (bi, 0, h)),
            pl.BlockSpec((None, s, HEAD_DIM), lambda bi, h: (bi, 0, n_kv_heads + h)),
            pl.BlockSpec((None, s, cw), lambda bi, h: (bi, 0, 0)),
            pl.BlockSpec((None, None, group, s), lambda bi, h: (bi, h, 0, 0)),
        ],
        out_specs=pl.BlockSpec((None, s, group * HEAD_DIM), lambda bi, h: (bi, 0, h)),
        out_shape=jax.ShapeDtypeStruct((b, s, dq), BF16),
        scratch_shapes=[stat, stat, stat, stat],
        compiler_params=_params("parallel", "parallel"),
        name="forgetting_attention",
    )(q, kv, kv, c_col, c_row)


def _rms_norm_kernel(h_ref, g_ref, o_ref):
    x = h_ref[...]
    ms = jnp.mean(x * x, axis=-1, keepdims=True)
    o_ref[...] = x * lax.rsqrt(ms + RMS_EPS) * g_ref[...]


def rms_norm(h, gain, *, tm_target=256):
    m, d = h.shape
    tm = _pick_tile(m, tm_target, V7X_SUBLANES)
    return pl.pallas_call(
        _rms_norm_kernel,
        grid=(m // tm,),
        in_specs=[pl.BlockSpec((tm, d), lambda i: (i, 0)),
                  pl.BlockSpec((1, d), lambda i: (0, 0))],
        out_specs=pl.BlockSpec((tm, d), lambda i: (i, 0)),
        out_shape=jax.ShapeDtypeStruct((m, d), F32),
        compiler_params=_params("parallel"),
        name="final_rms_norm",
    )(h, gain.reshape(1, d))


def _conv_glu_ffn(h, gain, w_up, w_conv, b_conv, w_down, seq):
    (xn,) = rms_norm_cast(h, [gain])
    act = ffn_up(xn, w_up.astype(BF16), w_conv, b_conv, seq)
    return matmul_residual(act, w_down.astype(BF16), h, tm_target=512, tn_target=512)


def kernel(x, a_attn_norm, a_w_qkv, a_w_o, a_ffn_norm, a_w_up, a_w_conv, a_b_conv, a_w_down,
           kv_norm, w_kv, w_f, b_f,
           b_attn_norm, b_w_q, b_w_o, b_ffn_norm, b_w_up, b_w_conv, b_b_conv, b_w_down,
           final_norm):
    bsz, seq, d = x.shape
    m = bsz * seq
    n_heads = d // HEAD_DIM
    n_kv_heads = w_kv.shape[1] // (2 * HEAD_DIM)
    group = n_heads // n_kv_heads
    n_gates = w_f.shape[1]
    assert n_gates == n_heads and n_gates <= V7X_LANES

    h = x.reshape(m, d)

    for i in range(a_w_qkv.shape[0]):
        (xn,) = rms_norm_cast(h, [a_attn_norm[i]])
        qkv = matmul(xn, a_w_qkv[i].astype(BF16), scaled_cols=d, scale=SB_Q_SCALE)
        o = stick_breaking_attention(qkv.reshape(bsz, seq, 3 * d), n_heads)
        h = matmul_residual(o.reshape(m, d), a_w_o[i].astype(BF16), h, tm_target=1024, tn_target=1024)
        h = _conv_glu_ffn(h, a_ffn_norm[i], a_w_up[i], a_w_conv[i], a_b_conv[i], a_w_down[i], seq)

    kv = c_col = c_row = None
    for j in range(b_w_q.shape[0]):
        if j == 0:
            xn_kv, xn = rms_norm_cast(h, [kv_norm, b_attn_norm[j]])
            wf_pad = jnp.pad(w_f, ((0, 0), (0, V7X_LANES - n_gates))).astype(BF16)
            bf_pad = jnp.pad(b_f, (0, V7X_LANES - n_gates)).reshape(1, V7X_LANES)
            kv, log_f = kv_proj(xn_kv, w_kv.astype(BF16), wf_pad, bf_pad)
            kv = kv.reshape(bsz, seq, 2 * n_kv_heads * HEAD_DIM)
            c_col = cumsum_positions(log_f.reshape(bsz, seq, V7X_LANES))
            c_row = (c_col[:, :, :n_gates].transpose(0, 2, 1)
                     .reshape(bsz, n_kv_heads, group, seq))
        else:
            (xn,) = rms_norm_cast(h, [b_attn_norm[j]])
        q = matmul(xn, b_w_q[j].astype(BF16), scaled_cols=d, scale=FOX_Q_SCALE)
        o = forgetting_attention(q.reshape(bsz, seq, d), kv, c_col, c_row, n_kv_heads)
        h = matmul_residual(o.reshape(m, d), b_w_o[j].astype(BF16), h, tm_target=1024, tn_target=1024)
        h = _conv_glu_ffn(h, b_ffn_norm[j], b_w_up[j], b_w_conv[j], b_b_conv[j], b_w_down[j], seq)

    return rms_norm(h, final_norm).reshape(bsz, seq, d)
```

```python
import functools
import math

import jax
import jax.numpy as jnp
from jax import lax
from jax.experimental import pallas as pl
from jax.experimental.pallas import tpu as pltpu

HEAD_DIM = 128
CONV_WIDTH = 3
RMS_EPS = 1e-6

V7X_LANES = 128
V7X_SUBLANES = 8
V7X_MXU_DIM = 256
V7X_VMEM_LIMIT_BYTES = 56 * 1024 * 1024

F32 = jnp.float32
BF16 = jnp.bfloat16

NORM_ROW_CHUNK = 32
ATTN_ROW_CHUNK = 32
MASK_VALUE = -1e30
LOG2_E = 1.4426950408889634
F32_SIGN_BIT = 0x80000000
FOX_Q_SCALE = HEAD_DIM ** -0.5 * LOG2_E
SB_Q_SCALE = -FOX_Q_SCALE


def _params(*semantics):
    return pltpu.CompilerParams(dimension_semantics=semantics,
                                vmem_limit_bytes=V7X_VMEM_LIMIT_BYTES)


def _pick_tile(n, target, quantum):
    if n <= target:
        return n
    t = (target // quantum) * quantum
    while t >= quantum:
        if n % t == 0:
            return t
        t -= quantum
    raise ValueError(f"no tile for {n}")


def _rms_norm_cast_kernel(h_ref, g_ref, *xn_refs):
    tm = h_ref.shape[0]

    def body(c, carry):
        r0 = pl.multiple_of(c * NORM_ROW_CHUNK, NORM_ROW_CHUNK)
        x = h_ref[pl.ds(r0, NORM_ROW_CHUNK), :]
        ms = jnp.mean(x * x, axis=-1, keepdims=True)
        y = x * lax.rsqrt(ms + RMS_EPS)
        for k, xn_ref in enumerate(xn_refs):
            xn_ref[pl.ds(r0, NORM_ROW_CHUNK), :] = (y * g_ref[k:k + 1, :]).astype(xn_ref.dtype)
        return carry

    lax.fori_loop(0, tm // NORM_ROW_CHUNK, body, 0, unroll=2)


def rms_norm_cast(h, gains, *, tm_target=512):
    m, d = h.shape
    ng = len(gains)
    tm = _pick_tile(m, tm_target, 2 * NORM_ROW_CHUNK)
    out = pl.pallas_call(
        _rms_norm_cast_kernel,
        grid=(m // tm,),
        in_specs=[pl.BlockSpec((tm, d), lambda i: (i, 0)),
                  pl.BlockSpec((ng, d), lambda i: (0, 0))],
        out_specs=[pl.BlockSpec((tm, d), lambda i: (i, 0))] * ng,
        out_shape=[jax.ShapeDtypeStruct((m, d), BF16)] * ng,
        compiler_params=_params("parallel"),
        name="rms_norm_cast",
    )(h, jnp.stack(gains))
    return list(out)


def _matmul_kernel(x_ref, w_ref, o_ref, *, scaled_tiles, scale):
    acc = jnp.dot(x_ref[...], w_ref[...], preferred_element_type=F32)
    if scaled_tiles:
        acc = acc * jnp.where(pl.program_id(1) < scaled_tiles, scale, 1.0)
    o_ref[...] = acc.astype(o_ref.dtype)


def matmul(x, w, *, scaled_cols=0, scale=1.0, tm_target=2048, tn_target=512):
    m, k = x.shape
    n = w.shape[1]
    tm = _pick_tile(m, tm_target, 2 * V7X_SUBLANES)
    tn = _pick_tile(n, tn_target, V7X_LANES)
    assert scaled_cols % tn == 0
    return pl.pallas_call(
        functools.partial(_matmul_kernel, scaled_tiles=scaled_cols // tn, scale=scale),
        grid=(m // tm, n // tn),
        in_specs=[
            pl.BlockSpec((tm, k), lambda i, j: (i, 0)),
            pl.BlockSpec((k, tn), lambda i, j: (0, j)),
        ],
        out_specs=pl.BlockSpec((tm, tn), lambda i, j: (i, j)),
        out_shape=jax.ShapeDtypeStruct((m, n), BF16),
        compiler_params=_params("parallel", "parallel"),
        name="matmul",
    )(x, w)


def _kv_proj_kernel(x_ref, w_ref, wf_ref, bf_ref, o_ref, lf_ref):
    @pl.when(pl.program_id(1) == 0)
    def _():
        pre = jnp.dot(x_ref[...], wf_ref[...], preferred_element_type=F32) + bf_ref[...]
        lf_ref[...] = -(jnp.maximum(-pre, 0.0) + jnp.log(1.0 + jnp.exp(-jnp.abs(pre))))

    o_ref[...] = jnp.dot(x_ref[...], w_ref[...], preferred_element_type=F32).astype(o_ref.dtype)


def kv_proj(x, w_kv, wf_pad, bf_pad, *, tm_target=1024, tn_target=1024):
    m, d = x.shape
    n = w_kv.shape[1]
    nf = wf_pad.shape[1]
    tm = _pick_tile(m, tm_target, 2 * V7X_SUBLANES)
    tn = _pick_tile(n, tn_target, V7X_LANES)
    return pl.pallas_call(
        _kv_proj_kernel,
        grid=(m // tm, n // tn),
        in_specs=[
            pl.BlockSpec((tm, d), lambda i, j: (i, 0)),
            pl.BlockSpec((d, tn), lambda i, j: (0, j)),
            pl.BlockSpec((d, nf), lambda i, j: (0, 0)),
            pl.BlockSpec((1, nf), lambda i, j: (0, 0)),
        ],
        out_specs=[
            pl.BlockSpec((tm, tn), lambda i, j: (i, j)),
            pl.BlockSpec((tm, nf), lambda i, j: (i, 0)),
        ],
        out_shape=[
            jax.ShapeDtypeStruct((m, n), BF16),
            jax.ShapeDtypeStruct((m, nf), F32),
        ],
        compiler_params=_params("parallel", "arbitrary"),
        name="kv_proj",
    )(x, w_kv, wf_pad, bf_pad)


def _split3(x):
    hi = x.astype(BF16)
    r1 = x - hi.astype(F32)
    mid = r1.astype(BF16)
    lo = (r1 - mid.astype(F32)).astype(BF16)
    return hi, mid, lo


def _cumsum_kernel(x_ref, tri_ref, c_ref, run_ref):
    blk = tri_ref.shape[0]
    nblk = x_ref.shape[0] // blk
    run_ref[...] = jnp.zeros_like(run_ref)

    def body(n, carry):
        r0 = pl.multiple_of(n * blk, blk)
        hi, mid, lo = _split3(x_ref[pl.ds(r0, blk), :])
        tri = tri_ref[...]
        incl = (jnp.dot(tri, hi, preferred_element_type=F32)
                + jnp.dot(tri, mid, preferred_element_type=F32)
                + jnp.dot(tri, lo, preferred_element_type=F32)
                + run_ref[0:1, :])
        c_ref[pl.ds(r0, blk), :] = incl
        run_ref[0:1, :] = incl[blk - 1:blk, :]
        return carry

    lax.fori_loop(0, nblk, body, 0)


def cumsum_positions(x):
    b, s, c = x.shape
    blk = V7X_LANES
    tri = (lax.broadcasted_iota(jnp.int32, (blk, blk), 1)
           <= lax.broadcasted_iota(jnp.int32, (blk, blk), 0)).astype(BF16)
    return pl.pallas_call(
        _cumsum_kernel,
        grid=(b,),
        in_specs=[
            pl.BlockSpec((None, s, c), lambda i: (i, 0, 0)),
            pl.BlockSpec((blk, blk), lambda i: (0, 0)),
        ],
        out_specs=pl.BlockSpec((None, s, c), lambda i: (i, 0, 0)),
        out_shape=jax.ShapeDtypeStruct((b, s, c), F32),
        scratch_shapes=[pltpu.VMEM((V7X_SUBLANES, c), F32)],
        compiler_params=_params("parallel"),
        name="cumsum_positions",
    )(x, tri)


def _matmul_residual_kernel(a_ref, w_ref, r_ref, o_ref):
    o_ref[...] = r_ref[...] + jnp.dot(a_ref[...], w_ref[...], preferred_element_type=F32)


def matmul_residual(a, w, res, *, tm_target, tn_target):
    m, k = a.shape
    n = w.shape[1]
    tm = _pick_tile(m, tm_target, 2 * V7X_SUBLANES)
    tn = _pick_tile(n, tn_target, V7X_LANES)
    return pl.pallas_call(
        _matmul_residual_kernel,
        grid=(m // tm, n // tn),
        in_specs=[
            pl.BlockSpec((tm, k), lambda i, j: (i, 0)),
            pl.BlockSpec((k, tn), lambda i, j: (0, j)),
            pl.BlockSpec((tm, tn), lambda i, j: (i, j)),
        ],
        out_specs=pl.BlockSpec((tm, tn), lambda i, j: (i, j)),
        out_shape=jax.ShapeDtypeStruct((m, n), F32),
        compiler_params=_params("parallel", "parallel"),
        name="matmul_residual",
    )(a, w, res)


def _ffn_up_kernel(xn_ref, wg_ref, wv_ref, cp_ref, o_ref, halo_ref, tail_ref, wb_ref,
                   *, tiles_per_seq, row_splits):
    i = pl.program_id(0)
    j = pl.program_id(1)
    tm, tn = o_ref.shape
    halo = V7X_SUBLANES
    mc = tm // row_splits

    seq_start = (i % tiles_per_seq) == 0

    @pl.when(seq_start)
    def _():
        halo_ref[...] = jnp.zeros_like(halo_ref)

    @pl.when(jnp.logical_not(seq_start))
    def _():
        halo_ref[...] = tail_ref[j]

    wb_ref[0] = wg_ref[...].astype(BF16)
    wb_ref[1] = wv_ref[...].astype(BF16)

    cp = cp_ref[j]
    prev_rows = halo_ref[...]
    for c in range(row_splits):
        rows = slice(c * mc, (c + 1) * mc)
        xc = xn_ref[rows, :]
        gate = jnp.dot(xc, wb_ref[0], preferred_element_type=F32)
        val = jnp.dot(xc, wb_ref[1], preferred_element_type=F32)
        ext = jnp.concatenate([prev_rows, gate], axis=0)
        g1 = pltpu.roll(ext, 1, 0)[halo:, :]
        g2 = pltpu.roll(ext, 2, 0)[halo:, :]
        conv = cp[3:4, :] + (cp[0:1, :] * g2 + cp[1:2, :] * g1 + cp[2:3, :] * gate)
        act = conv * (1.0 / (1.0 + jnp.exp(-conv))) * val
        o_ref[rows, :] = act.astype(o_ref.dtype)
        prev_rows = gate[mc - halo:mc, :]
    tail_ref[j] = prev_rows


def ffn_up(xn, w_up, w_conv, b_conv, seq, *, tm_target=2048, tn_target=256, chunk_rows=512):
    m, d = xn.shape
    f = w_up.shape[1] // 2
    tm = _pick_tile(seq, tm_target, 2 * V7X_SUBLANES)
    tn = _pick_tile(f, tn_target, V7X_LANES)
    nj = f // tn
    halo = V7X_SUBLANES
    row_splits = tm // chunk_rows if tm % chunk_rows == 0 else 1
    assert tm // row_splits >= halo and CONV_WIDTH - 1 <= halo and CONV_WIDTH + 1 <= V7X_SUBLANES
    conv_params = jnp.concatenate(
        [w_conv, b_conv.reshape(1, f),
         jnp.zeros((V7X_SUBLANES - CONV_WIDTH - 1, f), F32)], axis=0)
    conv_params = conv_params.reshape(V7X_SUBLANES, nj, tn).transpose(1, 0, 2)
    return pl.pallas_call(
        functools.partial(_ffn_up_kernel, tiles_per_seq=seq // tm, row_splits=row_splits),
        grid=(m // tm, nj),
        in_specs=[
            pl.BlockSpec((tm, d), lambda i, j: (i, 0)),
            pl.BlockSpec((d, tn), lambda i, j: (0, j)),
            pl.BlockSpec((d, tn), lambda i, j: (0, j + nj)),
            pl.BlockSpec((nj, V7X_SUBLANES, tn), lambda i, j: (0, 0, 0)),
        ],
        out_specs=pl.BlockSpec((tm, tn), lambda i, j: (i, j)),
        out_shape=jax.ShapeDtypeStruct((m, f), BF16),
        scratch_shapes=[
            pltpu.VMEM((halo, tn), F32),
            pltpu.VMEM((nj, halo, tn), F32),
            pltpu.VMEM((2, d, tn), BF16),
        ],
        compiler_params=_params("arbitrary", "arbitrary"),
        name="ffn_up",
    )(xn, w_up, w_up, conv_params)


def _sb_attn_kernel(q_ref, k_ref, v_ref, u_ref, o_ref,
                    acc_ref, run_ref, z_ref, lsm_ref, rsum_ref, incl_ref, w_ref, *, tq, tk, heads):
    seq = q_ref.shape[0]
    nsub = tq // tk
    reps = tk // V7X_LANES
    nchunk = tq // ATTN_ROW_CHUNK

    def strict_mask(r):
        shape = (ATTN_ROW_CHUNK, tq)
        return (lax.broadcasted_iota(jnp.int32, shape, 1)
                < r * ATTN_ROW_CHUNK + lax.broadcasted_iota(jnp.int32, shape, 0))

    def one_block(q0, kb, masked):
        k0 = pl.multiple_of(kb * tq, tq)
        for hh in range(heads):
            cols_h = slice(hh * HEAD_DIM, (hh + 1) * HEAD_DIM)
            z_ref[hh] = lax.dot_general(q_ref[pl.ds(q0, tq), cols_h], k_ref[pl.ds(k0, tq), cols_h],
                                        (((1,), (1,)), ((), ())), preferred_element_type=F32)
            for r in range(nchunk):
                rows = slice(r * ATTN_ROW_CHUNK, (r + 1) * ATTN_ROW_CHUNK)
                nz2 = z_ref[hh, rows, :]
                neg_abs = lax.bitcast_convert_type(
                    lax.bitcast_convert_type(nz2, jnp.uint32) | jnp.uint32(F32_SIGN_BIT), F32)
                lsm = jnp.minimum(nz2, 0.0) - jnp.log2(1.0 + jnp.exp2(neg_abs))
                if masked:
                    lsm = jnp.where(strict_mask(r), lsm, 0.0)
                lsm_ref[hh, rows, :] = lsm.astype(BF16)
                for c in range(nsub):
                    total = jnp.sum(lsm[:, c * tk:(c + 1) * tk], axis=-1, keepdims=True)
                    rsum_ref[hh, c, rows, :] = jnp.broadcast_to(total, (ATTN_ROW_CHUNK, V7X_LANES))
            for c in range(nsub):
                incl_ref[hh, c] = jnp.dot(lsm_ref[hh, :, c * tk:(c + 1) * tk], u_ref[...],
                                          preferred_element_type=F32)
            for r in range(nchunk):
                rows = slice(r * ATTN_ROW_CHUNK, (r + 1) * ATTN_ROW_CHUNK)
                run = run_ref[hh, rows, :]
                for c in reversed(range(nsub)):
                    cols = slice(c * tk, (c + 1) * tk)
                    e2 = incl_ref[hh, c, rows, :] + jnp.tile(run, (1, reps)) - z_ref[hh, rows, cols]
                    w = jnp.exp2(e2)
                    if masked:
                        w = jnp.where(strict_mask(r)[:, cols], w, 0.0)
                    w_ref[hh, rows, cols] = w.astype(BF16)
                    run = run + rsum_ref[hh, c, rows, :]
                run_ref[hh, rows, :] = run
            acc_ref[hh] += jnp.dot(w_ref[hh], v_ref[pl.ds(k0, tq), cols_h],
                                   preferred_element_type=F32)

    def q_body(qi, carry):
        q0 = pl.multiple_of(qi * tq, tq)
        acc_ref[...] = jnp.zeros_like(acc_ref)
        run_ref[...] = jnp.zeros_like(run_ref)
        one_block(q0, qi, True)

        def kb_body(n, c):
            one_block(q0, qi - 1 - n, False)
            return c

        lax.fori_loop(0, qi, kb_body, 0)
        for hh in range(heads):
            o_ref[pl.ds(q0, tq), hh * HEAD_DIM:(hh + 1) * HEAD_DIM] = acc_ref[hh].astype(o_ref.dtype)
        return carry

    lax.fori_loop(0, seq // tq, q_body, 0)


def stick_breaking_attention(qkv, n_heads, *, tq_target=512, tk=V7X_MXU_DIM, heads_per_step=4):
    b, s, _ = qkv.shape
    tk = min(tk, s)
    tq = _pick_tile(s, tq_target, tk)
    hps = heads_per_step if n_heads % heads_per_step == 0 else 1
    nstep = n_heads // hps
    width = hps * HEAD_DIM
    u = (lax.broadcasted_iota(jnp.int32, (tk, tk), 0)
         >= lax.broadcasted_iota(jnp.int32, (tk, tk), 1)).astype(BF16)
    kern = functools.partial(_sb_attn_kernel, tq=tq, tk=tk, heads=hps)
    return pl.pallas_call(
        kern,
        grid=(b, nstep),
        in_specs=[
            pl.BlockSpec((None, s, width), lambda bi, h: (bi, 0, h)),
            pl.BlockSpec((None, s, width), lambda bi, h: (bi, 0, nstep + h)),
            pl.BlockSpec((None, s, width), lambda bi, h: (bi, 0, 2 * nstep + h)),
            pl.BlockSpec((tk, tk), lambda bi, h: (0, 0)),
        ],
        out_specs=pl.BlockSpec((None, s, width), lambda bi, h: (bi, 0, h)),
        out_shape=jax.ShapeDtypeStruct((b, s, n_heads * HEAD_DIM), BF16),
        scratch_shapes=[
            pltpu.VMEM((hps, tq, HEAD_DIM), F32),
            pltpu.VMEM((hps, tq, V7X_LANES), F32),
            pltpu.VMEM((hps, tq, tq), F32),
            pltpu.VMEM((hps, tq, tq), BF16),
            pltpu.VMEM((hps, tq // tk, tq, V7X_LANES), F32),
            pltpu.VMEM((hps, tq // tk, tq, tk), F32),
            pltpu.VMEM((hps, tq, tq), BF16),
        ],
        compiler_params=_params("parallel", "parallel"),
        name="stick_breaking_attention",
    )(qkv, qkv, qkv, u)


def _fox_attn_kernel(q_ref, k_ref, v_ref, ccol_ref, crow_ref, o_ref,
                     acc_ref, m_ref, l_ref, bt_ref, *, tq, group):
    seq = k_ref.shape[0]
    kvh = pl.program_id(1)
    reps = tq // V7X_LANES
    heads_per_dot = 2 if group % 2 == 0 else 1

    def head_stack(q0, k0, g0, ng, masked):
        q = jnp.concatenate(
            [q_ref[pl.ds(q0, tq), g * HEAD_DIM:(g + 1) * HEAD_DIM] for g in range(g0, g0 + ng)],
            axis=0)
        s_val = lax.dot_general(q, k_ref[pl.ds(k0, tq), :],
                                (((1,), (1,)), ((), ())), preferred_element_type=F32)
        p_parts = []
        for g in range(g0, g0 + ng):
            cs = crow_ref[g:g + 1, pl.ds(k0, tq)] * LOG2_E
            for r in range(tq // ATTN_ROW_CHUNK):
                rows = slice(r * ATTN_ROW_CHUNK, (r + 1) * ATTN_ROW_CHUNK)
                srows = slice((g - g0) * tq + rows.start, (g - g0) * tq + rows.stop)
                grows = slice(g * tq + rows.start, g * tq + rows.stop)
                u = s_val[srows, :] - cs
                if masked:
                    shape = (ATTN_ROW_CHUNK, tq)
                    causal = (lax.broadcasted_iota(jnp.int32, shape, 1)
                              <= rows.start + lax.broadcasted_iota(jnp.int32, shape, 0))
                    u = jnp.where(causal, u, MASK_VALUE)
                bt = bt_ref[grows, :]
                m_old = m_ref[grows, :]
                m_new = jnp.maximum(m_old, jnp.max(u, axis=-1, keepdims=True) + bt)
                alpha = jnp.exp2(m_old - m_new)
                p = jnp.exp2(u - jnp.tile(m_new - bt, (1, reps)))
                l_ref[grows, :] = alpha * l_ref[grows, :] + jnp.sum(p, axis=-1, keepdims=True)
                m_ref[grows, :] = m_new
                acc_ref[grows, :] = alpha * acc_ref[grows, :]
                p_parts.append(p.astype(BF16))
        trows = slice(g0 * tq, (g0 + ng) * tq)
        acc_ref[trows, :] += jnp.dot(jnp.concatenate(p_parts, axis=0), v_ref[pl.ds(k0, tq), :],
                                     preferred_element_type=F32)

    def one_block(q0, kb, masked):
        k0 = pl.multiple_of(kb * tq, tq)
        for g0 in range(0, group, heads_per_dot):
            head_stack(q0, k0, g0, heads_per_dot, masked)

    def q_body(qi, carry):
        q0 = pl.multiple_of(qi * tq, tq)
        cblk = ccol_ref[pl.ds(q0, tq), :]
        lane = lax.broadcasted_iota(jnp.int32, cblk.shape, 1)
        for g in range(group):
            c_t = jnp.sum(jnp.where(lane == kvh * group + g, cblk, 0.0), axis=-1, keepdims=True)
            bt_ref[g * tq:(g + 1) * tq, :] = jnp.broadcast_to(c_t * LOG2_E, (tq, V7X_LANES))
        acc_ref[...] = jnp.zeros_like(acc_ref)
        l_ref[...] = jnp.zeros_like(l_ref)
        m_ref[...] = jnp.full_like(m_ref, MASK_VALUE)
        one_block(q0, qi, True)

        def kb_body(kb, c):
            one_block(q0, kb, False)
            return c

        lax.fori_loop(0, qi, kb_body, 0)
        out = acc_ref[...] / l_ref[...]
        for g in range(group):
            o_ref[pl.ds(q0, tq), g * HEAD_DIM:(g + 1) * HEAD_DIM] = (
                out[g * tq:(g + 1) * tq, :].astype(o_ref.dtype))
        return carry

    lax.fori_loop(0, seq // tq, q_body, 0)


def forgetting_attention(q, kv, c_col, c_row, n_kv_heads, *, tq_target=512):
    b, s, dq = q.shape
    assert HEAD_DIM == V7X_LANES
    group = dq // HEAD_DIM // n_kv_heads
    tq = _pick_tile(s, tq_target, V7X_LANES)
    cw = c_col.shape[2]
    kern = functools.partial(_fox_attn_kernel, tq=tq, group=group)
    stat = pltpu.VMEM((group * tq, V7X_LANES), F32)
    return pl.pallas_call(
        kern,
        grid=(b, n_kv_heads),
        in_specs=[
            pl.BlockSpec((None, s, group * HEAD_DIM), lambda bi, h: (bi, 0, h)),
            pl.BlockSpec((None, s, HEAD_DIM), lambda bi, h: (bi, 0, h)),
            pl.BlockSpec((None, s, HEAD_DIM), lambda bi, h: (bi, 0, n_kv_heads + h)),
            pl.BlockSpec((None, s, cw), lambda bi, h: (bi, 0, 0)),
            pl.BlockSpec((None, None, group, s), lambda bi, h: (bi, h, 0, 0)),
        ],
        out_specs=pl.BlockSpec((None, s, group * HEAD_DIM), lambda bi, h: (bi, 0, h)),
        out_shape=jax.ShapeDtypeStruct((b, s, dq), BF16),
        scratch_shapes=[stat, stat, stat, stat],
        compiler_params=_params("parallel", "parallel"),
        name="forgetting_attention",
    )(q, kv, kv, c_col, c_row)


def _rms_norm_kernel(h_ref, g_ref, o_ref):
    x = h_ref[...]
    ms = jnp.mean(x * x, axis=-1, keepdims=True)
    o_ref[...] = x * lax.rsqrt(ms + RMS_EPS) * g_ref[...]


def rms_norm(h, gain, *, tm_target=256):
    m, d = h.shape
    tm = _pick_tile(m, tm_target, V7X_SUBLANES)
    return pl.pallas_call(
        _rms_norm_kernel,
        grid=(m // tm,),
        in_specs=[pl.BlockSpec((tm, d), lambda i: (i, 0)),
                  pl.BlockSpec((1, d), lambda i: (0, 0))],
        out_specs=pl.BlockSpec((tm, d), lambda i: (i, 0)),
        out_shape=jax.ShapeDtypeStruct((m, d), F32),
        compiler_params=_params("parallel"),
        name="final_rms_norm",
    )(h, gain.reshape(1, d))


def _conv_glu_ffn(h, gain, w_up, w_conv, b_conv, w_down, seq):
    (xn,) = rms_norm_cast(h, [gain])
    act = ffn_up(xn, w_up, w_conv, b_conv, seq)
    return matmul_residual(act, w_down.astype(BF16), h, tm_target=512, tn_target=512)


def kernel(x, a_attn_norm, a_w_qkv, a_w_o, a_ffn_norm, a_w_up, a_w_conv, a_b_conv, a_w_down,
           kv_norm, w_kv, w_f, b_f,
           b_attn_norm, b_w_q, b_w_o, b_ffn_norm, b_w_up, b_w_conv, b_b_conv, b_w_down,
           final_norm):
    bsz, seq, d = x.shape
    m = bsz * seq
    n_heads = d // HEAD_DIM
    n_kv_heads = w_kv.shape[1] // (2 * HEAD_DIM)
    group = n_heads // n_kv_heads
    n_gates = w_f.shape[1]
    assert n_gates == n_heads and n_gates <= V7X_LANES

    h = x.reshape(m, d)

    for i in range(a_w_qkv.shape[0]):
        (xn,) = rms_norm_cast(h, [a_attn_norm[i]])
        qkv = matmul(xn, a_w_qkv[i].astype(BF16), scaled_cols=d, scale=SB_Q_SCALE)
        o = stick_breaking_attention(qkv.reshape(bsz, seq, 3 * d), n_heads)
        h = matmul_residual(o.reshape(m, d), a_w_o[i].astype(BF16), h, tm_target=1024, tn_target=1024)
        h = _conv_glu_ffn(h, a_ffn_norm[i], a_w_up[i], a_w_conv[i], a_b_conv[i], a_w_down[i], seq)

    kv = c_col = c_row = None
    for j in range(b_w_q.shape[0]):
        if j == 0:
            xn_kv, xn = rms_norm_cast(h, [kv_norm, b_attn_norm[j]])
            wf_pad = jnp.pad(w_f, ((0, 0), (0, V7X_LANES - n_gates))).astype(BF16)
            bf_pad = jnp.pad(b_f, (0, V7X_LANES - n_gates)).reshape(1, V7X_LANES)
            kv, log_f = kv_proj(xn_kv, w_kv.astype(BF16), wf_pad, bf_pad)
            kv = kv.reshape(bsz, seq, 2 * n_kv_heads * HEAD_DIM)
            c_col = cumsum_positions(log_f.reshape(bsz, seq, V7X_LANES))
            c_row = (c_col[:, :, :n_gates].transpose(0, 2, 1)
                     .reshape(bsz, n_kv_heads, group, seq))
        else:
            (xn,) = rms_norm_cast(h, [b_attn_norm[j]])
        q = matmul(xn, b_w_q[j].astype(BF16), scaled_cols=d, scale=FOX_Q_SCALE)
        o = forgetting_attention(q.reshape(bsz, seq, d), kv, c_col, c_row, n_kv_heads)
        h = matmul_residual(o.reshape(m, d), b_w_o[j].astype(BF16), h, tm_target=1024, tn_target=1024)
        h = _conv_glu_ffn(h, b_ffn_norm[j], b_w_up[j], b_w_conv[j], b_b_conv[j], b_w_down[j], seq)

    return rms_norm(h, final_norm).reshape(bsz, seq, d)
```

```python
import functools
import math

import jax
import jax.numpy as jnp
from jax import lax
from jax.experimental import pallas as pl
from jax.experimental.pallas import tpu as pltpu

HEAD_DIM = 128
CONV_WIDTH = 3
RMS_EPS = 1e-6

V7X_LANES = 128
V7X_SUBLANES = 8
V7X_MXU_DIM = 256
V7X_VMEM_LIMIT_BYTES = 56 * 1024 * 1024

F32 = jnp.float32
BF16 = jnp.bfloat16

NORM_ROW_CHUNK = 32
ATTN_ROW_CHUNK = 32
MASK_VALUE = -1e30
LOG2_E = 1.4426950408889634
F32_SIGN_BIT = 0x80000000
FOX_Q_SCALE = HEAD_DIM ** -0.5 * LOG2_E
SB_Q_SCALE = -FOX_Q_SCALE


def _params(*semantics):
    return pltpu.CompilerParams(dimension_semantics=semantics,
                                vmem_limit_bytes=V7X_VMEM_LIMIT_BYTES)


def _pick_tile(n, target, quantum):
    if n <= target:
        return n
    t = (target // quantum) * quantum
    while t >= quantum:
        if n % t == 0:
            return t
        t -= quantum
    raise ValueError(f"no tile for {n}")


def _rms_norm_cast_kernel(h_ref, g_ref, *xn_refs):
    tm = h_ref.shape[0]

    def body(c, carry):
        r0 = pl.multiple_of(c * NORM_ROW_CHUNK, NORM_ROW_CHUNK)
        x = h_ref[pl.ds(r0, NORM_ROW_CHUNK), :]
        ms = jnp.mean(x * x, axis=-1, keepdims=True)
        y = x * lax.rsqrt(ms + RMS_EPS)
        for k, xn_ref in enumerate(xn_refs):
            xn_ref[pl.ds(r0, NORM_ROW_CHUNK), :] = (y * g_ref[k:k + 1, :]).astype(xn_ref.dtype)
        return carry

    lax.fori_loop(0, tm // NORM_ROW_CHUNK, body, 0, unroll=2)


def rms_norm_cast(h, gains, *, tm_target=512):
    m, d = h.shape
    ng = len(gains)
    tm = _pick_tile(m, tm_target, 2 * NORM_ROW_CHUNK)
    out = pl.pallas_call(
        _rms_norm_cast_kernel,
        grid=(m // tm,),
        in_specs=[pl.BlockSpec((tm, d), lambda i: (i, 0)),
                  pl.BlockSpec((ng, d), lambda i: (0, 0))],
        out_specs=[pl.BlockSpec((tm, d), lambda i: (i, 0))] * ng,
        out_shape=[jax.ShapeDtypeStruct((m, d), BF16)] * ng,
        compiler_params=_params("parallel"),
        name="rms_norm_cast",
    )(h, jnp.stack(gains))
    return list(out)


def _matmul_kernel(x_ref, w_ref, o_ref, *, scaled_tiles, scale):
    acc = jnp.dot(x_ref[...], w_ref[...], preferred_element_type=F32)
    if scaled_tiles:
        acc = acc * jnp.where(pl.program_id(1) < scaled_tiles, scale, 1.0)
    o_ref[...] = acc.astype(o_ref.dtype)


def matmul(x, w, *, scaled_cols=0, scale=1.0, tm_target=2048, tn_target=512):
    m, k = x.shape
    n = w.shape[1]
    tm = _pick_tile(m, tm_target, 2 * V7X_SUBLANES)
    tn = _pick_tile(n, tn_target, V7X_LANES)
    assert scaled_cols % tn == 0
    return pl.pallas_call(
        functools.partial(_matmul_kernel, scaled_tiles=scaled_cols // tn, scale=scale),
        grid=(m // tm, n // tn),
        in_specs=[
            pl.BlockSpec((tm, k), lambda i, j: (i, 0)),
            pl.BlockSpec((k, tn), lambda i, j: (0, j)),
        ],
        out_specs=pl.BlockSpec((tm, tn), lambda i, j: (i, j)),
        out_shape=jax.ShapeDtypeStruct((m, n), BF16),
        compiler_params=_params("parallel", "parallel"),
        name="matmul",
    )(x, w)


def _kv_proj_kernel(x_ref, w_ref, wf_ref, bf_ref, o_ref, lf_ref):
    @pl.when(pl.program_id(1) == 0)
    def _():
        pre = jnp.dot(x_ref[...], wf_ref[...], preferred_element_type=F32) + bf_ref[...]
        lf_ref[...] = -(jnp.maximum(-pre, 0.0) + jnp.log(1.0 + jnp.exp(-jnp.abs(pre))))

    o_ref[...] = jnp.dot(x_ref[...], w_ref[...], preferred_element_type=F32).astype(o_ref.dtype)


def kv_proj(x, w_kv, wf_pad, bf_pad, *, tm_target=1024, tn_target=1024):
    m, d = x.shape
    n = w_kv.shape[1]
    nf = wf_pad.shape[1]
    tm = _pick_tile(m, tm_target, 2 * V7X_SUBLANES)
    tn = _pick_tile(n, tn_target, V7X_LANES)
    return pl.pallas_call(
        _kv_proj_kernel,
        grid=(m // tm, n // tn),
        in_specs=[
            pl.BlockSpec((tm, d), lambda i, j: (i, 0)),
            pl.BlockSpec((d, tn), lambda i, j: (0, j)),
            pl.BlockSpec((d, nf), lambda i, j: (0, 0)),
            pl.BlockSpec((1, nf), lambda i, j: (0, 0)),
        ],
        out_specs=[
            pl.BlockSpec((tm, tn), lambda i, j: (i, j)),
            pl.BlockSpec((tm, nf), lambda i, j: (i, 0)),
        ],
        out_shape=[
            jax.ShapeDtypeStruct((m, n), BF16),
            jax.ShapeDtypeStruct((m, nf), F32),
        ],
        compiler_params=_params("parallel", "arbitrary"),
        name="kv_proj",
    )(x, w_kv, wf_pad, bf_pad)


def _split3(x):
    hi = x.astype(BF16)
    r1 = x - hi.astype(F32)
    mid = r1.astype(BF16)
    lo = (r1 - mid.astype(F32)).astype(BF16)
    return hi, mid, lo


def _cumsum_kernel(x_ref, tri_ref, c_ref, run_ref):
    blk = tri_ref.shape[0]
    nblk = x_ref.shape[0] // blk
    run_ref[...] = jnp.zeros_like(run_ref)

    def body(n, carry):
        r0 = pl.multiple_of(n * blk, blk)
        hi, mid, lo = _split3(x_ref[pl.ds(r0, blk), :])
        tri = tri_ref[...]
        incl = (jnp.dot(tri, hi, preferred_element_type=F32)
                + jnp.dot(tri, mid, preferred_element_type=F32)
                + jnp.dot(tri, lo, preferred_element_type=F32)
                + run_ref[0:1, :])
        c_ref[pl.ds(r0, blk), :] = incl
        run_ref[0:1, :] = incl[blk - 1:blk, :]
        return carry

    lax.fori_loop(0, nblk, body, 0)


def cumsum_positions(x):
    b, s, c = x.shape
    blk = V7X_LANES
    tri = (lax.broadcasted_iota(jnp.int32, (blk, blk), 1)
           <= lax.broadcasted_iota(jnp.int32, (blk, blk), 0)).astype(BF16)
    return pl.pallas_call(
        _cumsum_kernel,
        grid=(b,),
        in_specs=[
            pl.BlockSpec((None, s, c), lambda i: (i, 0, 0)),
            pl.BlockSpec((blk, blk), lambda i: (0, 0)),
        ],
        out_specs=pl.BlockSpec((None, s, c), lambda i: (i, 0, 0)),
        out_shape=jax.ShapeDtypeStruct((b, s, c), F32),
        scratch_shapes=[pltpu.VMEM((V7X_SUBLANES, c), F32)],
        compiler_params=_params("parallel"),
        name="cumsum_positions",
    )(x, tri)


def _matmul_residual_kernel(a_ref, w_ref, r_ref, o_ref):
    o_ref[...] = r_ref[...] + jnp.dot(a_ref[...], w_ref[...], preferred_element_type=F32)


def matmul_residual(a, w, res, *, tm_target, tn_target):
    m, k = a.shape
    n = w.shape[1]
    tm = _pick_tile(m, tm_target, 2 * V7X_SUBLANES)
    tn = _pick_tile(n, tn_target, V7X_LANES)
    return pl.pallas_call(
        _matmul_residual_kernel,
        grid=(m // tm, n // tn),
        in_specs=[
            pl.BlockSpec((tm, k), lambda i, j: (i, 0)),
            pl.BlockSpec((k, tn), lambda i, j: (0, j)),
            pl.BlockSpec((tm, tn), lambda i, j: (i, j)),
        ],
        out_specs=pl.BlockSpec((tm, tn), lambda i, j: (i, j)),
        out_shape=jax.ShapeDtypeStruct((m, n), F32),
        compiler_params=_params("parallel", "parallel"),
        name="matmul_residual",
    )(a, w, res)


def _ffn_up_kernel(xn_ref, wg_ref, wv_ref, cp_ref, o_ref, halo_ref, tail_ref, wb_ref,
                   *, tiles_per_seq, row_splits):
    i = pl.program_id(0)
    j = pl.program_id(1)
    tm, tn = o_ref.shape
    halo = V7X_SUBLANES
    mc = tm // row_splits

    seq_start = (i % tiles_per_seq) == 0

    @pl.when(seq_start)
    def _():
        halo_ref[...] = jnp.zeros_like(halo_ref)

    @pl.when(jnp.logical_not(seq_start))
    def _():
        halo_ref[...] = tail_ref[j]

    wb_ref[0] = wg_ref[...].astype(BF16)
    wb_ref[1] = wv_ref[...].astype(BF16)

    cp = cp_ref[j]
    prev_rows = halo_ref[...]
    for c in range(row_splits):
        rows = slice(c * mc, (c + 1) * mc)
        xc = xn_ref[rows, :]
        gate = jnp.dot(xc, wb_ref[0], preferred_element_type=F32)
        val = jnp.dot(xc, wb_ref[1], preferred_element_type=F32)
        ext = jnp.concatenate([prev_rows, gate], axis=0)
        g1 = pltpu.roll(ext, 1, 0)[halo:, :]
        g2 = pltpu.roll(ext, 2, 0)[halo:, :]
        conv = cp[3:4, :] + (cp[0:1, :] * g2 + cp[1:2, :] * g1 + cp[2:3, :] * gate)
        act = conv * (1.0 / (1.0 + jnp.exp(-conv))) * val
        o_ref[rows, :] = act.astype(o_ref.dtype)
        prev_rows = gate[mc - halo:mc, :]
    tail_ref[j] = prev_rows


def ffn_up(xn, w_up, w_conv, b_conv, seq, *, tm_target=2048, tn_target=256, chunk_rows=512):
    m, d = xn.shape
    f = w_up.shape[1] // 2
    tm = _pick_tile(seq, tm_target, 2 * V7X_SUBLANES)
    tn = _pick_tile(f, tn_target, V7X_LANES)
    nj = f // tn
    halo = V7X_SUBLANES
    row_splits = tm // chunk_rows if tm % chunk_rows == 0 else 1
    assert tm // row_splits >= halo and CONV_WIDTH - 1 <= halo and CONV_WIDTH + 1 <= V7X_SUBLANES
    conv_params = jnp.concatenate(
        [w_conv, b_conv.reshape(1, f),
         jnp.zeros((V7X_SUBLANES - CONV_WIDTH - 1, f), F32)], axis=0)
    conv_params = conv_params.reshape(V7X_SUBLANES, nj, tn).transpose(1, 0, 2)
    return pl.pallas_call(
        functools.partial(_ffn_up_kernel, tiles_per_seq=seq // tm, row_splits=row_splits),
        grid=(m // tm, nj),
        in_specs=[
            pl.BlockSpec((tm, d), lambda i, j: (i, 0)),
            pl.BlockSpec((d, tn), lambda i, j: (0, j)),
            pl.BlockSpec((d, tn), lambda i, j: (0, j + nj)),
            pl.BlockSpec((nj, V7X_SUBLANES, tn), lambda i, j: (0, 0, 0)),
        ],
        out_specs=pl.BlockSpec((tm, tn), lambda i, j: (i, j)),
        out_shape=jax.ShapeDtypeStruct((m, f), BF16),
        scratch_shapes=[
            pltpu.VMEM((halo, tn), F32),
            pltpu.VMEM((nj, halo, tn), F32),
            pltpu.VMEM((2, d, tn), BF16),
        ],
        compiler_params=_params("arbitrary", "arbitrary"),
        name="ffn_up",
    )(xn, w_up, w_up, conv_params)


def _sb_attn_kernel(q_ref, k_ref, v_ref, u_ref, o_ref,
                    acc_ref, run_ref, z_ref, lsm_ref, rsum_ref, incl_ref, w_ref, *, tq, tk, heads):
    seq = q_ref.shape[0]
    nsub = tq // tk
    reps = tk // V7X_LANES
    nchunk = tq // ATTN_ROW_CHUNK

    def strict_mask(r):
        shape = (ATTN_ROW_CHUNK, tq)
        return (lax.broadcasted_iota(jnp.int32, shape, 1)
                < r * ATTN_ROW_CHUNK + lax.broadcasted_iota(jnp.int32, shape, 0))

    def one_block(q0, kb, masked):
        k0 = pl.multiple_of(kb * tq, tq)
        for hh in range(heads):
            cols_h = slice(hh * HEAD_DIM, (hh + 1) * HEAD_DIM)
            z_ref[hh] = lax.dot_general(q_ref[pl.ds(q0, tq), cols_h], k_ref[pl.ds(k0, tq), cols_h],
                                        (((1,), (1,)), ((), ())), preferred_element_type=F32)
            for r in range(nchunk):
                rows = slice(r * ATTN_ROW_CHUNK, (r + 1) * ATTN_ROW_CHUNK)
                nz2 = z_ref[hh, rows, :]
                neg_abs = lax.bitcast_convert_type(
                    lax.bitcast_convert_type(nz2, jnp.uint32) | jnp.uint32(F32_SIGN_BIT), F32)
                lsm = jnp.minimum(nz2, 0.0) - jnp.log2(1.0 + jnp.exp2(neg_abs))
                z_ref[hh, rows, :] = lsm - nz2
                if masked:
                    lsm = jnp.where(strict_mask(r), lsm, 0.0)
                lsm_ref[hh, rows, :] = lsm.astype(BF16)
                for c in range(nsub):
                    total = jnp.sum(lsm[:, c * tk:(c + 1) * tk], axis=-1, keepdims=True)
                    rsum_ref[hh, c, rows, :] = jnp.broadcast_to(total, (ATTN_ROW_CHUNK, V7X_LANES))
            for c in range(nsub):
                incl_ref[hh, c] = jnp.dot(lsm_ref[hh, :, c * tk:(c + 1) * tk], u_ref[...],
                                          preferred_element_type=F32)
            for r in range(nchunk):
                rows = slice(r * ATTN_ROW_CHUNK, (r + 1) * ATTN_ROW_CHUNK)
                run = run_ref[hh, rows, :]
                for c in reversed(range(nsub)):
                    cols = slice(c * tk, (c + 1) * tk)
                    e2 = incl_ref[hh, c, rows, :] + jnp.tile(run, (1, reps)) + z_ref[hh, rows, cols]
                    w = jnp.exp2(e2)
                    if masked:
                        w = jnp.where(strict_mask(r)[:, cols], w, 0.0)
                    w_ref[hh, rows, cols] = w.astype(BF16)
                    run = run + rsum_ref[hh, c, rows, :]
                run_ref[hh, rows, :] = run
            acc_ref[hh] += jnp.dot(w_ref[hh], v_ref[pl.ds(k0, tq), cols_h],
                                   preferred_element_type=F32)

    def q_body(qi, carry):
        q0 = pl.multiple_of(qi * tq, tq)
        acc_ref[...] = jnp.zeros_like(acc_ref)
        run_ref[...] = jnp.zeros_like(run_ref)
        one_block(q0, qi, True)

        def kb_body(n, c):
            one_block(q0, qi - 1 - n, False)
            return c

        lax.fori_loop(0, qi, kb_body, 0)
        for hh in range(heads):
            o_ref[pl.ds(q0, tq), hh * HEAD_DIM:(hh + 1) * HEAD_DIM] = acc_ref[hh].astype(o_ref.dtype)
        return carry

    lax.fori_loop(0, seq // tq, q_body, 0)


def stick_breaking_attention(qkv, n_heads, *, tq_target=512, tk=V7X_MXU_DIM, heads_per_step=4):
    b, s, _ = qkv.shape
    tk = min(tk, s)
    tq = _pick_tile(s, tq_target, tk)
    hps = heads_per_step if n_heads % heads_per_step == 0 else 1
    nstep = n_heads // hps
    width = hps * HEAD_DIM
    u = (lax.broadcasted_iota(jnp.int32, (tk, tk), 0)
         > lax.broadcasted_iota(jnp.int32, (tk, tk), 1)).astype(BF16)
    kern = functools.partial(_sb_attn_kernel, tq=tq, tk=tk, heads=hps)
    return pl.pallas_call(
        kern,
        grid=(b, nstep),
        in_specs=[
            pl.BlockSpec((None, s, width), lambda bi, h: (bi, 0, h)),
            pl.BlockSpec((None, s, width), lambda bi, h: (bi, 0, nstep + h)),
            pl.BlockSpec((None, s, width), lambda bi, h: (bi, 0, 2 * nstep + h)),
            pl.BlockSpec((tk, tk), lambda bi, h: (0, 0)),
        ],
        out_specs=pl.BlockSpec((None, s, width), lambda bi, h: (bi, 0, h)),
        out_shape=jax.ShapeDtypeStruct((b, s, n_heads * HEAD_DIM), BF16),
        scratch_shapes=[
            pltpu.VMEM((hps, tq, HEAD_DIM), F32),
            pltpu.VMEM((hps, tq, V7X_LANES), F32),
            pltpu.VMEM((hps, tq, tq), F32),
            pltpu.VMEM((hps, tq, tq), BF16),
            pltpu.VMEM((hps, tq // tk, tq, V7X_LANES), F32),
            pltpu.VMEM((hps, tq // tk, tq, tk), F32),
            pltpu.VMEM((hps, tq, tq), BF16),
        ],
        compiler_params=_params("parallel", "parallel"),
        name="stick_breaking_attention",
    )(qkv, qkv, qkv, u)


def _fox_attn_kernel(q_ref, k_ref, v_ref, ccol_ref, crow_ref, o_ref,
                     acc_ref, m_ref, l_ref, bt_ref, *, tq, group):
    seq = k_ref.shape[0]
    kvh = pl.program_id(1)
    reps = tq // V7X_LANES
    heads_per_dot = 2 if group % 2 == 0 else 1

    def head_stack(q0, k0, g0, ng, masked):
        q = jnp.concatenate(
            [q_ref[pl.ds(q0, tq), g * HEAD_DIM:(g + 1) * HEAD_DIM] for g in range(g0, g0 + ng)],
            axis=0)
        s_val = lax.dot_general(q, k_ref[pl.ds(k0, tq), :],
                                (((1,), (1,)), ((), ())), preferred_element_type=F32)
        p_parts = []
        for g in range(g0, g0 + ng):
            cs = crow_ref[g:g + 1, pl.ds(k0, tq)] * LOG2_E
            for r in range(tq // ATTN_ROW_CHUNK):
                rows = slice(r * ATTN_ROW_CHUNK, (r + 1) * ATTN_ROW_CHUNK)
                srows = slice((g - g0) * tq + rows.start, (g - g0) * tq + rows.stop)
                grows = slice(g * tq + rows.start, g * tq + rows.stop)
                u = s_val[srows, :] - cs
                if masked:
                    shape = (ATTN_ROW_CHUNK, tq)
                    causal = (lax.broadcasted_iota(jnp.int32, shape, 1)
                              <= rows.start + lax.broadcasted_iota(jnp.int32, shape, 0))
                    u = jnp.where(causal, u, MASK_VALUE)
                bt = bt_ref[grows, :]
                m_old = m_ref[grows, :]
                m_new = jnp.maximum(m_old, jnp.max(u, axis=-1, keepdims=True) + bt)
                alpha = jnp.exp2(m_old - m_new)
                p = jnp.exp2(u - jnp.tile(m_new - bt, (1, reps)))
                l_ref[grows, :] = alpha * l_ref[grows, :] + jnp.sum(p, axis=-1, keepdims=True)
                m_ref[grows, :] = m_new
                acc_ref[grows, :] = alpha * acc_ref[grows, :]
                p_parts.append(p.astype(BF16))
        trows = slice(g0 * tq, (g0 + ng) * tq)
        acc_ref[trows, :] += jnp.dot(jnp.concatenate(p_parts, axis=0), v_ref[pl.ds(k0, tq), :],
                                     preferred_element_type=F32)

    def one_block(q0, kb, masked):
        k0 = pl.multiple_of(kb * tq, tq)
        for g0 in range(0, group, heads_per_dot):
            head_stack(q0, k0, g0, heads_per_dot, masked)

    def q_body(qi, carry):
        q0 = pl.multiple_of(qi * tq, tq)
        cblk = ccol_ref[pl.ds(q0, tq), :]
        lane = lax.broadcasted_iota(jnp.int32, cblk.shape, 1)
        for g in range(group):
            c_t = jnp.sum(jnp.where(lane == kvh * group + g, cblk, 0.0), axis=-1, keepdims=True)
            bt_ref[g * tq:(g + 1) * tq, :] = jnp.broadcast_to(c_t * LOG2_E, (tq, V7X_LANES))
        acc_ref[...] = jnp.zeros_like(acc_ref)
        l_ref[...] = jnp.zeros_like(l_ref)
        m_ref[...] = jnp.full_like(m_ref, MASK_VALUE)
        one_block(q0, qi, True)

        def kb_body(kb, c):
            one_block(q0, kb, False)
            return c

        lax.fori_loop(0, qi, kb_body, 0)
        out = acc_ref[...] / l_ref[...]
        for g in range(group):
            o_ref[pl.ds(q0, tq), g * HEAD_DIM:(g + 1) * HEAD_DIM] = (
                out[g * tq:(g + 1) * tq, :].astype(o_ref.dtype))
        return carry

    lax.fori_loop(0, seq // tq, q_body, 0)


def forgetting_attention(q, kv, c_col, c_row, n_kv_heads, *, tq_target=512):
    b, s, dq = q.shape
    assert HEAD_DIM == V7X_LANES
    group = dq // HEAD_DIM // n_kv_heads
    tq = _pick_tile(s, tq_target, V7X_LANES)
    cw = c_col.shape[2]
    kern = functools.partial(_fox_attn_kernel, tq=tq, group=group)
    stat = pltpu.VMEM((group * tq, V7X_LANES), F32)
    return pl.pallas_call(
        kern,
        grid=(b, n_kv_heads),
        in_specs=[
            pl.BlockSpec((None, s, group * HEAD_DIM), lambda bi, h: (bi, 0, h)),
            pl.BlockSpec((None, s, HEAD_DIM), lambda bi, h: (bi, 0, h)),
            pl.BlockSpec((None, s, HEAD_DIM), lambda bi, h: (bi, 0, n_kv_heads + h)),
            pl.BlockSpec((None, s, cw), lambda bi, h: (bi, 0, 0)),
            pl.BlockSpec((None, None, group, s), lambda bi, h: (bi, h, 0, 0)),
        ],
        out_specs=pl.BlockSpec((None, s, group * HEAD_DIM), lambda bi, h: (bi, 0, h)),
        out_shape=jax.ShapeDtypeStruct((b, s, dq), BF16),
        scratch_shapes=[stat, stat, stat, stat],
        compiler_params=_params("parallel", "parallel"),
        name="forgetting_attention",
    )(q, kv, kv, c_col, c_row)


def _rms_norm_kernel(h_ref, g_ref, o_ref):
    x = h_ref[...]
    ms = jnp.mean(x * x, axis=-1, keepdims=True)
    o_ref[...] = x * lax.rsqrt(ms + RMS_EPS) * g_ref[...]


def rms_norm(h, gain, *, tm_target=256):
    m, d = h.shape
    tm = _pick_tile(m, tm_target, V7X_SUBLANES)
    return pl.pallas_call(
        _rms_norm_kernel,
        grid=(m // tm,),
        in_specs=[pl.BlockSpec((tm, d), lambda i: (i, 0)),
                  pl.BlockSpec((1, d), lambda i: (0, 0))],
        out_specs=pl.BlockSpec((tm, d), lambda i: (i, 0)),
        out_shape=jax.ShapeDtypeStruct((m, d), F32),
        compiler_params=_params("parallel"),
        name="final_rms_norm",
    )(h, gain.reshape(1, d))


def _conv_glu_ffn(h, gain, w_up, w_conv, b_conv, w_down, seq):
    (xn,) = rms_norm_cast(h, [gain])
    act = ffn_up(xn, w_up, w_conv, b_conv, seq)
    return matmul_residual(act, w_down.astype(BF16), h, tm_target=512, tn_target=512)


def kernel(x, a_attn_norm, a_w_qkv, a_w_o, a_ffn_norm, a_w_up, a_w_conv, a_b_conv, a_w_down,
           kv_norm, w_kv, w_f, b_f,
           b_attn_norm, b_w_q, b_w_o, b_ffn_norm, b_w_up, b_w_conv, b_b_conv, b_w_down,
           final_norm):
    bsz, seq, d = x.shape
    m = bsz * seq
    n_heads = d // HEAD_DIM
    n_kv_heads = w_kv.shape[1] // (2 * HEAD_DIM)
    group = n_heads // n_kv_heads
    n_gates = w_f.shape[1]
    assert n_gates == n_heads and n_gates <= V7X_LANES

    h = x.reshape(m, d)

    for i in range(a_w_qkv.shape[0]):
        (xn,) = rms_norm_cast(h, [a_attn_norm[i]])
        qkv = matmul(xn, a_w_qkv[i].astype(BF16), scaled_cols=d, scale=SB_Q_SCALE)
        o = stick_breaking_attention(qkv.reshape(bsz, seq, 3 * d), n_heads)
        h = matmul_residual(o.reshape(m, d), a_w_o[i].astype(BF16), h, tm_target=1024, tn_target=1024)
        h = _conv_glu_ffn(h, a_ffn_norm[i], a_w_up[i], a_w_conv[i], a_b_conv[i], a_w_down[i], seq)

    kv = c_col = c_row = None
    for j in range(b_w_q.shape[0]):
        if j == 0:
            xn_kv, xn = rms_norm_cast(h, [kv_norm, b_attn_norm[j]])
            wf_pad = jnp.pad(w_f, ((0, 0), (0, V7X_LANES - n_gates))).astype(BF16)
            bf_pad = jnp.pad(b_f, (0, V7X_LANES - n_gates)).reshape(1, V7X_LANES)
            kv, log_f = kv_proj(xn_kv, w_kv.astype(BF16), wf_pad, bf_pad)
            kv = kv.reshape(bsz, seq, 2 * n_kv_heads * HEAD_DIM)
            c_col = cumsum_positions(log_f.reshape(bsz, seq, V7X_LANES))
            c_row = (c_col[:, :, :n_gates].transpose(0, 2, 1)
                     .reshape(bsz, n_kv_heads, group, seq))
        else:
            (xn,) = rms_norm_cast(h, [b_attn_norm[j]])
        q = matmul(xn, b_w_q[j].astype(BF16), scaled_cols=d, scale=FOX_Q_SCALE)
        o = forgetting_attention(q.reshape(bsz, seq, d), kv, c_col, c_row, n_kv_heads)
        h = matmul_residual(o.reshape(m, d), b_w_o[j].astype(BF16), h, tm_target=1024, tn_target=1024)
        h = _conv_glu_ffn(h, b_ffn_norm[j], b_w_up[j], b_w_conv[j], b_b_conv[j], b_w_down[j], seq)

    return rms_norm(h, final_norm).reshape(bsz, seq, d)
```

```python
import functools
import math

import jax
import jax.numpy as jnp
from jax import lax
from jax.experimental import pallas as pl
from jax.experimental.pallas import tpu as pltpu

HEAD_DIM = 128
CONV_WIDTH = 3
RMS_EPS = 1e-6

V7X_LANES = 128
V7X_SUBLANES = 8
V7X_MXU_DIM = 256
V7X_VMEM_LIMIT_BYTES = 56 * 1024 * 1024

F32 = jnp.float32
BF16 = jnp.bfloat16

NORM_ROW_CHUNK = 32
ATTN_ROW_CHUNK = 32
MASK_VALUE = -1e30
LOG2_E = 1.4426950408889634
F32_SIGN_BIT = 0x80000000
FOX_Q_SCALE = HEAD_DIM ** -0.5 * LOG2_E
SB_Q_SCALE = -FOX_Q_SCALE


def _params(*semantics):
    return pltpu.CompilerParams(dimension_semantics=semantics,
                                vmem_limit_bytes=V7X_VMEM_LIMIT_BYTES)


def _pick_tile(n, target, quantum):
    if n <= target:
        return n
    t = (target // quantum) * quantum
    while t >= quantum:
        if n % t == 0:
            return t
        t -= quantum
    raise ValueError(f"no tile for {n}")


def _rms_norm_cast_kernel(h_ref, g_ref, *xn_refs):
    tm = h_ref.shape[0]

    def body(c, carry):
        r0 = pl.multiple_of(c * NORM_ROW_CHUNK, NORM_ROW_CHUNK)
        x = h_ref[pl.ds(r0, NORM_ROW_CHUNK), :]
        ms = jnp.mean(x * x, axis=-1, keepdims=True)
        y = x * lax.rsqrt(ms + RMS_EPS)
        for k, xn_ref in enumerate(xn_refs):
            xn_ref[pl.ds(r0, NORM_ROW_CHUNK), :] = (y * g_ref[k:k + 1, :]).astype(xn_ref.dtype)
        return carry

    lax.fori_loop(0, tm // NORM_ROW_CHUNK, body, 0, unroll=2)


def rms_norm_cast(h, gains, *, tm_target=512):
    m, d = h.shape
    ng = len(gains)
    tm = _pick_tile(m, tm_target, 2 * NORM_ROW_CHUNK)
    out = pl.pallas_call(
        _rms_norm_cast_kernel,
        grid=(m // tm,),
        in_specs=[pl.BlockSpec((tm, d), lambda i: (i, 0)),
                  pl.BlockSpec((ng, d), lambda i: (0, 0))],
        out_specs=[pl.BlockSpec((tm, d), lambda i: (i, 0))] * ng,
        out_shape=[jax.ShapeDtypeStruct((m, d), BF16)] * ng,
        compiler_params=_params("parallel"),
        name="rms_norm_cast",
    )(h, jnp.stack(gains))
    return list(out)


def _matmul_kernel(x_ref, w_ref, o_ref, *, scaled_tiles, scale):
    acc = jnp.dot(x_ref[...], w_ref[...], preferred_element_type=F32)
    if scaled_tiles:
        acc = acc * jnp.where(pl.program_id(1) < scaled_tiles, scale, 1.0)
    o_ref[...] = acc.astype(o_ref.dtype)


def matmul(x, w, *, scaled_cols=0, scale=1.0, tm_target=2048, tn_target=512):
    m, k = x.shape
    n = w.shape[1]
    tm = _pick_tile(m, tm_target, 2 * V7X_SUBLANES)
    tn = _pick_tile(n, tn_target, V7X_LANES)
    assert scaled_cols % tn == 0
    return pl.pallas_call(
        functools.partial(_matmul_kernel, scaled_tiles=scaled_cols // tn, scale=scale),
        grid=(m // tm, n // tn),
        in_specs=[
            pl.BlockSpec((tm, k), lambda i, j: (i, 0)),
            pl.BlockSpec((k, tn), lambda i, j: (0, j)),
        ],
        out_specs=pl.BlockSpec((tm, tn), lambda i, j: (i, j)),
        out_shape=jax.ShapeDtypeStruct((m, n), BF16),
        compiler_params=_params("parallel", "parallel"),
        name="matmul",
    )(x, w)


def _kv_proj_kernel(x_ref, w_ref, wf_ref, bf_ref, o_ref, lf_ref):
    @pl.when(pl.program_id(1) == 0)
    def _():
        pre = jnp.dot(x_ref[...], wf_ref[...], preferred_element_type=F32) + bf_ref[...]
        lf_ref[...] = -(jnp.maximum(-pre, 0.0) + jnp.log(1.0 + jnp.exp(-jnp.abs(pre))))

    o_ref[...] = jnp.dot(x_ref[...], w_ref[...], preferred_element_type=F32).astype(o_ref.dtype)


def kv_proj(x, w_kv, wf_pad, bf_pad, *, tm_target=1024, tn_target=1024):
    m, d = x.shape
    n = w_kv.shape[1]
    nf = wf_pad.shape[1]
    tm = _pick_tile(m, tm_target, 2 * V7X_SUBLANES)
    tn = _pick_tile(n, tn_target, V7X_LANES)
    return pl.pallas_call(
        _kv_proj_kernel,
        grid=(m // tm, n // tn),
        in_specs=[
            pl.BlockSpec((tm, d), lambda i, j: (i, 0)),
            pl.BlockSpec((d, tn), lambda i, j: (0, j)),
            pl.BlockSpec((d, nf), lambda i, j: (0, 0)),
            pl.BlockSpec((1, nf), lambda i, j: (0, 0)),
        ],
        out_specs=[
            pl.BlockSpec((tm, tn), lambda i, j: (i, j)),
            pl.BlockSpec((tm, nf), lambda i, j: (i, 0)),
        ],
        out_shape=[
            jax.ShapeDtypeStruct((m, n), BF16),
            jax.ShapeDtypeStruct((m, nf), F32),
        ],
        compiler_params=_params("parallel", "arbitrary"),
        name="kv_proj",
    )(x, w_kv, wf_pad, bf_pad)


def _split3(x):
    hi = x.astype(BF16)
    r1 = x - hi.astype(F32)
    mid = r1.astype(BF16)
    lo = (r1 - mid.astype(F32)).astype(BF16)
    return hi, mid, lo


def _cumsum_kernel(x_ref, tri_ref, c_ref, run_ref):
    blk = tri_ref.shape[0]
    nblk = x_ref.shape[0] // blk
    run_ref[...] = jnp.zeros_like(run_ref)

    def body(n, carry):
        r0 = pl.multiple_of(n * blk, blk)
        hi, mid, lo = _split3(x_ref[pl.ds(r0, blk), :])
        tri = tri_ref[...]
        incl = (jnp.dot(tri, hi, preferred_element_type=F32)
                + jnp.dot(tri, mid, preferred_element_type=F32)
                + jnp.dot(tri, lo, preferred_element_type=F32)
                + run_ref[0:1, :])
        c_ref[pl.ds(r0, blk), :] = incl
        run_ref[0:1, :] = incl[blk - 1:blk, :]
        return carry

    lax.fori_loop(0, nblk, body, 0)


def cumsum_positions(x):
    b, s, c = x.shape
    blk = V7X_LANES
    tri = (lax.broadcasted_iota(jnp.int32, (blk, blk), 1)
           <= lax.broadcasted_iota(jnp.int32, (blk, blk), 0)).astype(BF16)
    return pl.pallas_call(
        _cumsum_kernel,
        grid=(b,),
        in_specs=[
            pl.BlockSpec((None, s, c), lambda i: (i, 0, 0)),
            pl.BlockSpec((blk, blk), lambda i: (0, 0)),
        ],
        out_specs=pl.BlockSpec((None, s, c), lambda i: (i, 0, 0)),
        out_shape=jax.ShapeDtypeStruct((b, s, c), F32),
        scratch_shapes=[pltpu.VMEM((V7X_SUBLANES, c), F32)],
        compiler_params=_params("parallel"),
        name="cumsum_positions",
    )(x, tri)


def _matmul_residual_kernel(a_ref, w_ref, r_ref, o_ref):
    o_ref[...] = r_ref[...] + jnp.dot(a_ref[...], w_ref[...], preferred_element_type=F32)


def matmul_residual(a, w, res, *, tm_target, tn_target):
    m, k = a.shape
    n = w.shape[1]
    tm = _pick_tile(m, tm_target, 2 * V7X_SUBLANES)
    tn = _pick_tile(n, tn_target, V7X_LANES)
    return pl.pallas_call(
        _matmul_residual_kernel,
        grid=(m // tm, n // tn),
        in_specs=[
            pl.BlockSpec((tm, k), lambda i, j: (i, 0)),
            pl.BlockSpec((k, tn), lambda i, j: (0, j)),
            pl.BlockSpec((tm, tn), lambda i, j: (i, j)),
        ],
        out_specs=pl.BlockSpec((tm, tn), lambda i, j: (i, j)),
        out_shape=jax.ShapeDtypeStruct((m, n), F32),
        compiler_params=_params("parallel", "parallel"),
        name="matmul_residual",
    )(a, w, res)


def _ffn_up_kernel(xn_ref, wg_ref, wv_ref, cp_ref, o_ref, halo_ref, tail_ref, wb_ref,
                   *, tiles_per_seq, row_splits):
    i = pl.program_id(0)
    j = pl.program_id(1)
    tm, tn = o_ref.shape
    halo = V7X_SUBLANES
    mc = tm // row_splits

    seq_start = (i % tiles_per_seq) == 0

    @pl.when(seq_start)
    def _():
        halo_ref[...] = jnp.zeros_like(halo_ref)

    @pl.when(jnp.logical_not(seq_start))
    def _():
        halo_ref[...] = tail_ref[j]

    wb_ref[0] = wg_ref[...].astype(BF16)
    wb_ref[1] = wv_ref[...].astype(BF16)

    cp = cp_ref[j]
    prev_rows = halo_ref[...]
    for c in range(row_splits):
        rows = slice(c * mc, (c + 1) * mc)
        xc = xn_ref[rows, :]
        gate = jnp.dot(xc, wb_ref[0], preferred_element_type=F32)
        val = jnp.dot(xc, wb_ref[1], preferred_element_type=F32)
        ext = jnp.concatenate([prev_rows, gate], axis=0)
        g1 = pltpu.roll(ext, 1, 0)[halo:, :]
        g2 = pltpu.roll(ext, 2, 0)[halo:, :]
        conv = cp[3:4, :] + (cp[0:1, :] * g2 + cp[1:2, :] * g1 + cp[2:3, :] * gate)
        act = conv * (1.0 / (1.0 + jnp.exp(-conv))) * val
        o_ref[rows, :] = act.astype(o_ref.dtype)
        prev_rows = gate[mc - halo:mc, :]
    tail_ref[j] = prev_rows


def ffn_up(xn, w_up, w_conv, b_conv, seq, *, tm_target=2048, tn_target=256, chunk_rows=512):
    m, d = xn.shape
    f = w_up.shape[1] // 2
    tm = _pick_tile(seq, tm_target, 2 * V7X_SUBLANES)
    tn = _pick_tile(f, tn_target, V7X_LANES)
    nj = f // tn
    halo = V7X_SUBLANES
    row_splits = tm // chunk_rows if tm % chunk_rows == 0 else 1
    assert tm // row_splits >= halo and CONV_WIDTH - 1 <= halo and CONV_WIDTH + 1 <= V7X_SUBLANES
    conv_params = jnp.concatenate(
        [w_conv, b_conv.reshape(1, f),
         jnp.zeros((V7X_SUBLANES - CONV_WIDTH - 1, f), F32)], axis=0)
    conv_params = conv_params.reshape(V7X_SUBLANES, nj, tn).transpose(1, 0, 2)
    return pl.pallas_call(
        functools.partial(_ffn_up_kernel, tiles_per_seq=seq // tm, row_splits=row_splits),
        grid=(m // tm, nj),
        in_specs=[
            pl.BlockSpec((tm, d), lambda i, j: (i, 0)),
            pl.BlockSpec((d, tn), lambda i, j: (0, j)),
            pl.BlockSpec((d, tn), lambda i, j: (0, j + nj)),
            pl.BlockSpec((nj, V7X_SUBLANES, tn), lambda i, j: (0, 0, 0)),
        ],
        out_specs=pl.BlockSpec((tm, tn), lambda i, j: (i, j)),
        out_shape=jax.ShapeDtypeStruct((m, f), BF16),
        scratch_shapes=[
            pltpu.VMEM((halo, tn), F32),
            pltpu.VMEM((nj, halo, tn), F32),
            pltpu.VMEM((2, d, tn), BF16),
        ],
        compiler_params=_params("arbitrary", "arbitrary"),
        name="ffn_up",
    )(xn, w_up, w_up, conv_params)


def _sb_attn_kernel(q_ref, k_ref, v_ref, u_ref, o_ref,
                    acc_ref, run_ref, z_ref, own_ref, lsm_ref, rsum_ref, incl_ref, w_ref,
                    *, tq, tk, heads):
    seq = q_ref.shape[0]
    nsub = tq // tk
    reps = tk // V7X_LANES
    nchunk = tq // ATTN_ROW_CHUNK

    def strict_mask(r):
        shape = (ATTN_ROW_CHUNK, tq)
        return (lax.broadcasted_iota(jnp.int32, shape, 1)
                < r * ATTN_ROW_CHUNK + lax.broadcasted_iota(jnp.int32, shape, 0))

    def one_block(q0, kb, masked):
        k0 = pl.multiple_of(kb * tq, tq)
        for hh in range(heads):
            cols_h = slice(hh * HEAD_DIM, (hh + 1) * HEAD_DIM)
            z_ref[hh] = lax.dot_general(q_ref[pl.ds(q0, tq), cols_h], k_ref[pl.ds(k0, tq), cols_h],
                                        (((1,), (1,)), ((), ())), preferred_element_type=F32)
            for r in range(nchunk):
                rows = slice(r * ATTN_ROW_CHUNK, (r + 1) * ATTN_ROW_CHUNK)
                nz2 = z_ref[hh, rows, :]
                neg_abs = lax.bitcast_convert_type(
                    lax.bitcast_convert_type(nz2, jnp.uint32) | jnp.uint32(F32_SIGN_BIT), F32)
                lsm = jnp.minimum(nz2, 0.0) - jnp.log2(1.0 + jnp.exp2(neg_abs))
                own_ref[hh, rows, :] = lsm - nz2
                if masked:
                    lsm = jnp.where(strict_mask(r), lsm, 0.0)
                lsm_ref[hh, rows, :] = lsm.astype(BF16)
                for c in range(nsub):
                    total = jnp.sum(lsm[:, c * tk:(c + 1) * tk], axis=-1, keepdims=True)
                    rsum_ref[hh, c, rows, :] = jnp.broadcast_to(total, (ATTN_ROW_CHUNK, V7X_LANES))
            for c in range(nsub):
                incl_ref[hh, c] = jnp.dot(lsm_ref[hh, :, c * tk:(c + 1) * tk], u_ref[...],
                                          preferred_element_type=F32)
            for r in range(nchunk):
                rows = slice(r * ATTN_ROW_CHUNK, (r + 1) * ATTN_ROW_CHUNK)
                run = run_ref[hh, rows, :]
                for c in reversed(range(nsub)):
                    cols = slice(c * tk, (c + 1) * tk)
                    e2 = incl_ref[hh, c, rows, :] + jnp.tile(run, (1, reps)) + own_ref[hh, rows, cols]
                    w = jnp.exp2(e2)
                    if masked:
                        w = jnp.where(strict_mask(r)[:, cols], w, 0.0)
                    w_ref[hh, rows, cols] = w.astype(BF16)
                    run = run + rsum_ref[hh, c, rows, :]
                run_ref[hh, rows, :] = run
            acc_ref[hh] += jnp.dot(w_ref[hh], v_ref[pl.ds(k0, tq), cols_h],
                                   preferred_element_type=F32)

    def q_body(qi, carry):
        q0 = pl.multiple_of(qi * tq, tq)
        acc_ref[...] = jnp.zeros_like(acc_ref)
        run_ref[...] = jnp.zeros_like(run_ref)
        one_block(q0, qi, True)

        def kb_body(n, c):
            one_block(q0, qi - 1 - n, False)
            return c

        lax.fori_loop(0, qi, kb_body, 0)
        for hh in range(heads):
            o_ref[pl.ds(q0, tq), hh * HEAD_DIM:(hh + 1) * HEAD_DIM] = acc_ref[hh].astype(o_ref.dtype)
        return carry

    lax.fori_loop(0, seq // tq, q_body, 0)


def stick_breaking_attention(qkv, n_heads, *, tq_target=512, tk=V7X_MXU_DIM, heads_per_step=4):
    b, s, _ = qkv.shape
    tk = min(tk, s)
    tq = _pick_tile(s, tq_target, tk)
    hps = heads_per_step if n_heads % heads_per_step == 0 else 1
    nstep = n_heads // hps
    width = hps * HEAD_DIM
    u = (lax.broadcasted_iota(jnp.int32, (tk, tk), 0)
         > lax.broadcasted_iota(jnp.int32, (tk, tk), 1)).astype(BF16)
    kern = functools.partial(_sb_attn_kernel, tq=tq, tk=tk, heads=hps)
    return pl.pallas_call(
        kern,
        grid=(b, nstep),
        in_specs=[
            pl.BlockSpec((None, s, width), lambda bi, h: (bi, 0, h)),
            pl.BlockSpec((None, s, width), lambda bi, h: (bi, 0, nstep + h)),
            pl.BlockSpec((None, s, width), lambda bi, h: (bi, 0, 2 * nstep + h)),
            pl.BlockSpec((tk, tk), lambda bi, h: (0, 0)),
        ],
        out_specs=pl.BlockSpec((None, s, width), lambda bi, h: (bi, 0, h)),
        out_shape=jax.ShapeDtypeStruct((b, s, n_heads * HEAD_DIM), BF16),
        scratch_shapes=[
            pltpu.VMEM((hps, tq, HEAD_DIM), F32),
            pltpu.VMEM((hps, tq, V7X_LANES), F32),
            pltpu.VMEM((hps, tq, tq), F32),
            pltpu.VMEM((hps, tq, tq), F32),
            pltpu.VMEM((hps, tq, tq), BF16),
            pltpu.VMEM((hps, tq // tk, tq, V7X_LANES), F32),
            pltpu.VMEM((hps, tq // tk, tq, tk), F32),
            pltpu.VMEM((hps, tq, tq), BF16),
        ],
        compiler_params=_params("parallel", "parallel"),
        name="stick_breaking_attention",
    )(qkv, qkv, qkv, u)


def _fox_attn_kernel(q_ref, k_ref, v_ref, ccol_ref, crow_ref, o_ref,
                     acc_ref, m_ref, l_ref, bt_ref, *, tq, group):
    seq = k_ref.shape[0]
    kvh = pl.program_id(1)
    reps = tq // V7X_LANES
    heads_per_dot = 2 if group % 2 == 0 else 1

    def head_stack(q0, k0, g0, ng, masked):
        q = jnp.concatenate(
            [q_ref[pl.ds(q0, tq), g * HEAD_DIM:(g + 1) * HEAD_DIM] for g in range(g0, g0 + ng)],
            axis=0)
        s_val = lax.dot_general(q, k_ref[pl.ds(k0, tq), :],
                                (((1,), (1,)), ((), ())), preferred_element_type=F32)
        p_parts = []
        for g in range(g0, g0 + ng):
            cs = crow_ref[g:g + 1, pl.ds(k0, tq)] * LOG2_E
            for r in range(tq // ATTN_ROW_CHUNK):
                rows = slice(r * ATTN_ROW_CHUNK, (r + 1) * ATTN_ROW_CHUNK)
                srows = slice((g - g0) * tq + rows.start, (g - g0) * tq + rows.stop)
                grows = slice(g * tq + rows.start, g * tq + rows.stop)
                u = s_val[srows, :] - cs
                if masked:
                    shape = (ATTN_ROW_CHUNK, tq)
                    causal = (lax.broadcasted_iota(jnp.int32, shape, 1)
                              <= rows.start + lax.broadcasted_iota(jnp.int32, shape, 0))
                    u = jnp.where(causal, u, MASK_VALUE)
                bt = bt_ref[grows, :]
                m_old = m_ref[grows, :]
                m_new = jnp.maximum(m_old, jnp.max(u, axis=-1, keepdims=True) + bt)
                alpha = jnp.exp2(m_old - m_new)
                p = jnp.exp2(u - jnp.tile(m_new - bt, (1, reps)))
                l_ref[grows, :] = alpha * l_ref[grows, :] + jnp.sum(p, axis=-1, keepdims=True)
                m_ref[grows, :] = m_new
                acc_ref[grows, :] = alpha * acc_ref[grows, :]
                p_parts.append(p.astype(BF16))
        trows = slice(g0 * tq, (g0 + ng) * tq)
        acc_ref[trows, :] += jnp.dot(jnp.concatenate(p_parts, axis=0), v_ref[pl.ds(k0, tq), :],
                                     preferred_element_type=F32)

    def one_block(q0, kb, masked):
        k0 = pl.multiple_of(kb * tq, tq)
        for g0 in range(0, group, heads_per_dot):
            head_stack(q0, k0, g0, heads_per_dot, masked)

    def q_body(qi, carry):
        q0 = pl.multiple_of(qi * tq, tq)
        cblk = ccol_ref[pl.ds(q0, tq), :]
        lane = lax.broadcasted_iota(jnp.int32, cblk.shape, 1)
        for g in range(group):
            c_t = jnp.sum(jnp.where(lane == kvh * group + g, cblk, 0.0), axis=-1, keepdims=True)
            bt_ref[g * tq:(g + 1) * tq, :] = jnp.broadcast_to(c_t * LOG2_E, (tq, V7X_LANES))
        acc_ref[...] = jnp.zeros_like(acc_ref)
        l_ref[...] = jnp.zeros_like(l_ref)
        m_ref[...] = jnp.full_like(m_ref, MASK_VALUE)
        one_block(q0, qi, True)

        def kb_body(kb, c):
            one_block(q0, kb, False)
            return c

        lax.fori_loop(0, qi, kb_body, 0)
        out = acc_ref[...] / l_ref[...]
        for g in range(group):
            o_ref[pl.ds(q0, tq), g * HEAD_DIM:(g + 1) * HEAD_DIM] = (
                out[g * tq:(g + 1) * tq, :].astype(o_ref.dtype))
        return carry

    lax.fori_loop(0, seq // tq, q_body, 0)


def forgetting_attention(q, kv, c_col, c_row, n_kv_heads, *, tq_target=512):
    b, s, dq = q.shape
    assert HEAD_DIM == V7X_LANES
    group = dq // HEAD_DIM // n_kv_heads
    tq = _pick_tile(s, tq_target, V7X_LANES)
    cw = c_col.shape[2]
    kern = functools.partial(_fox_attn_kernel, tq=tq, group=group)
    stat = pltpu.VMEM((group * tq, V7X_LANES), F32)
    return pl.pallas_call(
        kern,
        grid=(b, n_kv_heads),
        in_specs=[
            pl.BlockSpec((None, s, group * HEAD_DIM), lambda bi, h: (bi, 0, h)),
            pl.BlockSpec((None, s, HEAD_DIM), lambda bi, h: (bi, 0, h)),
            pl.BlockSpec((None, s, HEAD_DIM), lambda bi, h: (bi, 0, n_kv_heads + h)),
            pl.BlockSpec((None, s, cw), lambda bi, h: (bi, 0, 0)),
            pl.BlockSpec((None, None, group, s), lambda bi, h: (bi, h, 0, 0)),
        ],
        out_specs=pl.BlockSpec((None, s, group * HEAD_DIM), lambda bi, h: (bi, 0, h)),
        out_shape=jax.ShapeDtypeStruct((b, s, dq), BF16),
        scratch_shapes=[stat, stat, stat, stat],
        compiler_params=_params("parallel", "parallel"),
        name="forgetting_attention",
    )(q, kv, kv, c_col, c_row)


def _rms_norm_kernel(h_ref, g_ref, o_ref):
    x = h_ref[...]
    ms = jnp.mean(x * x, axis=-1, keepdims=True)
    o_ref[...] = x * lax.rsqrt(ms + RMS_EPS) * g_ref[...]


def rms_norm(h, gain, *, tm_target=256):
    m, d = h.shape
    tm = _pick_tile(m, tm_target, V7X_SUBLANES)
    return pl.pallas_call(
        _rms_norm_kernel,
        grid=(m // tm,),
        in_specs=[pl.BlockSpec((tm, d), lambda i: (i, 0)),
                  pl.BlockSpec((1, d), lambda i: (0, 0))],
        out_specs=pl.BlockSpec((tm, d), lambda i: (i, 0)),
        out_shape=jax.ShapeDtypeStruct((m, d), F32),
        compiler_params=_params("parallel"),
        name="final_rms_norm",
    )(h, gain.reshape(1, d))


def _conv_glu_ffn(h, gain, w_up, w_conv, b_conv, w_down, seq):
    (xn,) = rms_norm_cast(h, [gain])
    act = ffn_up(xn, w_up, w_conv, b_conv, seq)
    return matmul_residual(act, w_down.astype(BF16), h, tm_target=512, tn_target=512)


def kernel(x, a_attn_norm, a_w_qkv, a_w_o, a_ffn_norm, a_w_up, a_w_conv, a_b_conv, a_w_down,
           kv_norm, w_kv, w_f, b_f,
           b_attn_norm, b_w_q, b_w_o, b_ffn_norm, b_w_up, b_w_conv, b_b_conv, b_w_down,
           final_norm):
    bsz, seq, d = x.shape
    m = bsz * seq
    n_heads = d // HEAD_DIM
    n_kv_heads = w_kv.shape[1] // (2 * HEAD_DIM)
    group = n_heads // n_kv_heads
    n_gates = w_f.shape[1]
    assert n_gates == n_heads and n_gates <= V7X_LANES

    h = x.reshape(m, d)

    for i in range(a_w_qkv.shape[0]):
        (xn,) = rms_norm_cast(h, [a_attn_norm[i]])
        qkv = matmul(xn, a_w_qkv[i].astype(BF16), scaled_cols=d, scale=SB_Q_SCALE)
        o = stick_breaking_attention(qkv.reshape(bsz, seq, 3 * d), n_heads)
        h = matmul_residual(o.reshape(m, d), a_w_o[i].astype(BF16), h, tm_target=1024, tn_target=1024)
        h = _conv_glu_ffn(h, a_ffn_norm[i], a_w_up[i], a_w_conv[i], a_b_conv[i], a_w_down[i], seq)

    kv = c_col = c_row = None
    for j in range(b_w_q.shape[0]):
        if j == 0:
            xn_kv, xn = rms_norm_cast(h, [kv_norm, b_attn_norm[j]])
            wf_pad = jnp.pad(w_f, ((0, 0), (0, V7X_LANES - n_gates))).astype(BF16)
            bf_pad = jnp.pad(b_f, (0, V7X_LANES - n_gates)).reshape(1, V7X_LANES)
            kv, log_f = kv_proj(xn_kv, w_kv.astype(BF16), wf_pad, bf_pad)
            kv = kv.reshape(bsz, seq, 2 * n_kv_heads * HEAD_DIM)
            c_col = cumsum_positions(log_f.reshape(bsz, seq, V7X_LANES))
            c_row = (c_col[:, :, :n_gates].transpose(0, 2, 1)
                     .reshape(bsz, n_kv_heads, group, seq))
        else:
            (xn,) = rms_norm_cast(h, [b_attn_norm[j]])
        q = matmul(xn, b_w_q[j].astype(BF16), scaled_cols=d, scale=FOX_Q_SCALE)
        o = forgetting_attention(q.reshape(bsz, seq, d), kv, c_col, c_row, n_kv_heads)
        h = matmul_residual(o.reshape(m, d), b_w_o[j].astype(BF16), h, tm_target=1024, tn_target=1024)
        h = _conv_glu_ffn(h, b_ffn_norm[j], b_w_up[j], b_w_conv[j], b_b_conv[j], b_w_down[j], seq)

    return rms_norm(h, final_norm).reshape(bsz, seq, d)
```

```python
import functools
import math

import jax
import jax.numpy as jnp
from jax import lax
from jax.experimental import pallas as pl
from jax.experimental.pallas import tpu as pltpu

HEAD_DIM = 128
CONV_WIDTH = 3
RMS_EPS = 1e-6

V7X_LANES = 128
V7X_SUBLANES = 8
V7X_MXU_DIM = 256
V7X_VMEM_LIMIT_BYTES = 56 * 1024 * 1024

F32 = jnp.float32
BF16 = jnp.bfloat16

NORM_ROW_CHUNK = 32
ATTN_ROW_CHUNK = 32
MASK_VALUE = -1e30
LOG2_E = 1.4426950408889634
F32_SIGN_BIT = 0x80000000
FOX_Q_SCALE = HEAD_DIM ** -0.5 * LOG2_E
SB_Q_SCALE = -FOX_Q_SCALE


def _params(*semantics):
    return pltpu.CompilerParams(dimension_semantics=semantics,
                                vmem_limit_bytes=V7X_VMEM_LIMIT_BYTES)


def _pick_tile(n, target, quantum):
    if n <= target:
        return n
    t = (target // quantum) * quantum
    while t >= quantum:
        if n % t == 0:
            return t
        t -= quantum
    raise ValueError(f"no tile for {n}")


def _rms_norm_cast_kernel(h_ref, g_ref, *xn_refs):
    tm = h_ref.shape[0]

    def body(c, carry):
        r0 = pl.multiple_of(c * NORM_ROW_CHUNK, NORM_ROW_CHUNK)
        x = h_ref[pl.ds(r0, NORM_ROW_CHUNK), :]
        ms = jnp.mean(x * x, axis=-1, keepdims=True)
        y = x * lax.rsqrt(ms + RMS_EPS)
        for k, xn_ref in enumerate(xn_refs):
            xn_ref[pl.ds(r0, NORM_ROW_CHUNK), :] = (y * g_ref[k:k + 1, :]).astype(xn_ref.dtype)
        return carry

    lax.fori_loop(0, tm // NORM_ROW_CHUNK, body, 0, unroll=2)


def rms_norm_cast(h, gains, *, tm_target=512):
    m, d = h.shape
    ng = len(gains)
    tm = _pick_tile(m, tm_target, 2 * NORM_ROW_CHUNK)
    out = pl.pallas_call(
        _rms_norm_cast_kernel,
        grid=(m // tm,),
        in_specs=[pl.BlockSpec((tm, d), lambda i: (i, 0)),
                  pl.BlockSpec((ng, d), lambda i: (0, 0))],
        out_specs=[pl.BlockSpec((tm, d), lambda i: (i, 0))] * ng,
        out_shape=[jax.ShapeDtypeStruct((m, d), BF16)] * ng,
        compiler_params=_params("parallel"),
        name="rms_norm_cast",
    )(h, jnp.stack(gains))
    return list(out)


def _matmul_kernel(x_ref, w_ref, o_ref, *, scaled_tiles, scale):
    acc = jnp.dot(x_ref[...], w_ref[...], preferred_element_type=F32)
    if scaled_tiles:
        acc = acc * jnp.where(pl.program_id(1) < scaled_tiles, scale, 1.0)
    o_ref[...] = acc.astype(o_ref.dtype)


def matmul(x, w, *, scaled_cols=0, scale=1.0, tm_target=2048, tn_target=512):
    m, k = x.shape
    n = w.shape[1]
    tm = _pick_tile(m, tm_target, 2 * V7X_SUBLANES)
    tn = _pick_tile(n, tn_target, V7X_LANES)
    assert scaled_cols % tn == 0
    return pl.pallas_call(
        functools.partial(_matmul_kernel, scaled_tiles=scaled_cols // tn, scale=scale),
        grid=(m // tm, n // tn),
        in_specs=[
            pl.BlockSpec((tm, k), lambda i, j: (i, 0)),
            pl.BlockSpec((k, tn), lambda i, j: (0, j)),
        ],
        out_specs=pl.BlockSpec((tm, tn), lambda i, j: (i, j)),
        out_shape=jax.ShapeDtypeStruct((m, n), BF16),
        compiler_params=_params("parallel", "parallel"),
        name="matmul",
    )(x, w)


def _kv_proj_kernel(x_ref, w_ref, wf_ref, bf_ref, o_ref, lf_ref):
    @pl.when(pl.program_id(1) == 0)
    def _():
        pre = jnp.dot(x_ref[...], wf_ref[...], preferred_element_type=F32) + bf_ref[...]
        lf_ref[...] = -(jnp.maximum(-pre, 0.0) + jnp.log(1.0 + jnp.exp(-jnp.abs(pre))))

    o_ref[...] = jnp.dot(x_ref[...], w_ref[...], preferred_element_type=F32).astype(o_ref.dtype)


def kv_proj(x, w_kv, wf_pad, bf_pad, *, tm_target=1024, tn_target=1024):
    m, d = x.shape
    n = w_kv.shape[1]
    nf = wf_pad.shape[1]
    tm = _pick_tile(m, tm_target, 2 * V7X_SUBLANES)
    tn = _pick_tile(n, tn_target, V7X_LANES)
    return pl.pallas_call(
        _kv_proj_kernel,
        grid=(m // tm, n // tn),
        in_specs=[
            pl.BlockSpec((tm, d), lambda i, j: (i, 0)),
            pl.BlockSpec((d, tn), lambda i, j: (0, j)),
            pl.BlockSpec((d, nf), lambda i, j: (0, 0)),
            pl.BlockSpec((1, nf), lambda i, j: (0, 0)),
        ],
        out_specs=[
            pl.BlockSpec((tm, tn), lambda i, j: (i, j)),
            pl.BlockSpec((tm, nf), lambda i, j: (i, 0)),
        ],
        out_shape=[
            jax.ShapeDtypeStruct((m, n), BF16),
            jax.ShapeDtypeStruct((m, nf), F32),
        ],
        compiler_params=_params("parallel", "arbitrary"),
        name="kv_proj",
    )(x, w_kv, wf_pad, bf_pad)


def _split3(x):
    hi = x.astype(BF16)
    r1 = x - hi.astype(F32)
    mid = r1.astype(BF16)
    lo = (r1 - mid.astype(F32)).astype(BF16)
    return hi, mid, lo


def _cumsum_kernel(x_ref, tri_ref, c_ref, run_ref):
    blk = tri_ref.shape[0]
    nblk = x_ref.shape[0] // blk
    run_ref[...] = jnp.zeros_like(run_ref)

    def body(n, carry):
        r0 = pl.multiple_of(n * blk, blk)
        hi, mid, lo = _split3(x_ref[pl.ds(r0, blk), :])
        tri = tri_ref[...]
        incl = (jnp.dot(tri, hi, preferred_element_type=F32)
                + jnp.dot(tri, mid, preferred_element_type=F32)
                + jnp.dot(tri, lo, preferred_element_type=F32)
                + run_ref[0:1, :])
        c_ref[pl.ds(r0, blk), :] = incl
        run_ref[0:1, :] = incl[blk - 1:blk, :]
        return carry

    lax.fori_loop(0, nblk, body, 0)


def cumsum_positions(x):
    b, s, c = x.shape
    blk = V7X_LANES
    tri = (lax.broadcasted_iota(jnp.int32, (blk, blk), 1)
           <= lax.broadcasted_iota(jnp.int32, (blk, blk), 0)).astype(BF16)
    return pl.pallas_call(
        _cumsum_kernel,
        grid=(b,),
        in_specs=[
            pl.BlockSpec((None, s, c), lambda i: (i, 0, 0)),
            pl.BlockSpec((blk, blk), lambda i: (0, 0)),
        ],
        out_specs=pl.BlockSpec((None, s, c), lambda i: (i, 0, 0)),
        out_shape=jax.ShapeDtypeStruct((b, s, c), F32),
        scratch_shapes=[pltpu.VMEM((V7X_SUBLANES, c), F32)],
        compiler_params=_params("parallel"),
        name="cumsum_positions",
    )(x, tri)


def _matmul_residual_kernel(a_ref, w_ref, r_ref, o_ref):
    o_ref[...] = r_ref[...] + jnp.dot(a_ref[...], w_ref[...], preferred_element_type=F32)


def matmul_residual(a, w, res, *, tm_target, tn_target):
    m, k = a.shape
    n = w.shape[1]
    tm = _pick_tile(m, tm_target, 2 * V7X_SUBLANES)
    tn = _pick_tile(n, tn_target, V7X_LANES)
    return pl.pallas_call(
        _matmul_residual_kernel,
        grid=(m // tm, n // tn),
        in_specs=[
            pl.BlockSpec((tm, k), lambda i, j: (i, 0)),
            pl.BlockSpec((k, tn), lambda i, j: (0, j)),
            pl.BlockSpec((tm, tn), lambda i, j: (i, j)),
        ],
        out_specs=pl.BlockSpec((tm, tn), lambda i, j: (i, j)),
        out_shape=jax.ShapeDtypeStruct((m, n), F32),
        compiler_params=_params("parallel", "parallel"),
        name="matmul_residual",
    )(a, w, res)


def _ffn_up_kernel(xn_ref, wg_ref, wv_ref, cp_ref, o_ref, halo_ref, tail_ref, wb_ref,
                   *, tiles_per_seq, row_splits):
    i = pl.program_id(0)
    j = pl.program_id(1)
    tm, tn = o_ref.shape
    halo = V7X_SUBLANES
    mc = tm // row_splits

    seq_start = (i % tiles_per_seq) == 0

    @pl.when(seq_start)
    def _():
        halo_ref[...] = jnp.zeros_like(halo_ref)

    @pl.when(jnp.logical_not(seq_start))
    def _():
        halo_ref[...] = tail_ref[j]

    wb_ref[0] = wg_ref[...].astype(BF16)
    wb_ref[1] = wv_ref[...].astype(BF16)

    cp = cp_ref[j]
    prev_rows = halo_ref[...]
    for c in range(row_splits):
        rows = slice(c * mc, (c + 1) * mc)
        xc = xn_ref[rows, :]
        gate = jnp.dot(xc, wb_ref[0], preferred_element_type=F32)
        val = jnp.dot(xc, wb_ref[1], preferred_element_type=F32)
        ext = jnp.concatenate([prev_rows, gate], axis=0)
        g1 = pltpu.roll(ext, 1, 0)[halo:, :]
        g2 = pltpu.roll(ext, 2, 0)[halo:, :]
        conv = cp[3:4, :] + (cp[0:1, :] * g2 + cp[1:2, :] * g1 + cp[2:3, :] * gate)
        act = conv * (1.0 / (1.0 + jnp.exp(-conv))) * val
        o_ref[rows, :] = act.astype(o_ref.dtype)
        prev_rows = gate[mc - halo:mc, :]
    tail_ref[j] = prev_rows


def ffn_up(xn, w_up, w_conv, b_conv, seq, *, tm_target=2048, tn_target=256, chunk_rows=512):
    m, d = xn.shape
    f = w_up.shape[1] // 2
    tm = _pick_tile(seq, tm_target, 2 * V7X_SUBLANES)
    tn = _pick_tile(f, tn_target, V7X_LANES)
    nj = f // tn
    halo = V7X_SUBLANES
    row_splits = tm // chunk_rows if tm % chunk_rows == 0 else 1
    assert tm // row_splits >= halo and CONV_WIDTH - 1 <= halo and CONV_WIDTH + 1 <= V7X_SUBLANES
    conv_params = jnp.concatenate(
        [w_conv, b_conv.reshape(1, f),
         jnp.zeros((V7X_SUBLANES - CONV_WIDTH - 1, f), F32)], axis=0)
    conv_params = conv_params.reshape(V7X_SUBLANES, nj, tn).transpose(1, 0, 2)
    return pl.pallas_call(
        functools.partial(_ffn_up_kernel, tiles_per_seq=seq // tm, row_splits=row_splits),
        grid=(m // tm, nj),
        in_specs=[
            pl.BlockSpec((tm, d), lambda i, j: (i, 0)),
            pl.BlockSpec((d, tn), lambda i, j: (0, j)),
            pl.BlockSpec((d, tn), lambda i, j: (0, j + nj)),
            pl.BlockSpec((nj, V7X_SUBLANES, tn), lambda i, j: (0, 0, 0)),
        ],
        out_specs=pl.BlockSpec((tm, tn), lambda i, j: (i, j)),
        out_shape=jax.ShapeDtypeStruct((m, f), BF16),
        scratch_shapes=[
            pltpu.VMEM((halo, tn), F32),
            pltpu.VMEM((nj, halo, tn), F32),
            pltpu.VMEM((2, d, tn), BF16),
        ],
        compiler_params=_params("arbitrary", "arbitrary"),
        name="ffn_up",
    )(xn, w_up, w_up, conv_params)


def _sb_attn_kernel(q_ref, k_ref, v_ref, u_ref, o_ref,
                    acc_ref, run_ref, z_ref, own_ref, lsm_ref, incl_ref, w_ref,
                    *, tq, tk, heads):
    seq = q_ref.shape[0]
    nsub = tq // tk
    reps = tk // V7X_LANES
    nchunk = tq // ATTN_ROW_CHUNK

    def strict_mask(r):
        shape = (ATTN_ROW_CHUNK, tq)
        return (lax.broadcasted_iota(jnp.int32, shape, 1)
                < r * ATTN_ROW_CHUNK + lax.broadcasted_iota(jnp.int32, shape, 0))

    def one_block(q0, kb, masked):
        k0 = pl.multiple_of(kb * tq, tq)
        for hh in range(heads):
            cols_h = slice(hh * HEAD_DIM, (hh + 1) * HEAD_DIM)
            z_ref[hh] = lax.dot_general(q_ref[pl.ds(q0, tq), cols_h], k_ref[pl.ds(k0, tq), cols_h],
                                        (((1,), (1,)), ((), ())), preferred_element_type=F32)
            for r in range(nchunk):
                rows = slice(r * ATTN_ROW_CHUNK, (r + 1) * ATTN_ROW_CHUNK)
                nz2 = z_ref[hh, rows, :]
                neg_abs = lax.bitcast_convert_type(
                    lax.bitcast_convert_type(nz2, jnp.uint32) | jnp.uint32(F32_SIGN_BIT), F32)
                lsm = jnp.minimum(nz2, 0.0) - jnp.log2(1.0 + jnp.exp2(neg_abs))
                own_ref[hh, rows, :] = lsm - nz2
                if masked:
                    lsm = jnp.where(strict_mask(r), lsm, 0.0)
                lsm_ref[hh, rows, :] = lsm.astype(BF16)
            for c in range(nsub):
                incl_ref[hh, c] = jnp.dot(lsm_ref[hh, :, c * tk:(c + 1) * tk], u_ref[...],
                                          preferred_element_type=F32)
            for r in range(nchunk):
                rows = slice(r * ATTN_ROW_CHUNK, (r + 1) * ATTN_ROW_CHUNK)
                run = run_ref[hh, rows, :]
                for c in reversed(range(nsub)):
                    cols = slice(c * tk, (c + 1) * tk)
                    excl = incl_ref[hh, c, rows, :]
                    e2 = excl + jnp.tile(run, (1, reps)) + own_ref[hh, rows, cols]
                    w = jnp.exp2(e2)
                    if masked:
                        w = jnp.where(strict_mask(r)[:, cols], w, 0.0)
                    w_ref[hh, rows, cols] = w.astype(BF16)
                    first = lsm_ref[hh, rows, c * tk:c * tk + V7X_LANES].astype(F32)
                    run = run + jnp.broadcast_to(excl[:, 0:1] + first[:, 0:1], run.shape)
                run_ref[hh, rows, :] = run
            acc_ref[hh] += jnp.dot(w_ref[hh], v_ref[pl.ds(k0, tq), cols_h],
                                   preferred_element_type=F32)

    def q_body(qi, carry):
        q0 = pl.multiple_of(qi * tq, tq)
        acc_ref[...] = jnp.zeros_like(acc_ref)
        run_ref[...] = jnp.zeros_like(run_ref)
        one_block(q0, qi, True)

        def kb_body(n, c):
            one_block(q0, qi - 1 - n, False)
            return c

        lax.fori_loop(0, qi, kb_body, 0)
        for hh in range(heads):
            o_ref[pl.ds(q0, tq), hh * HEAD_DIM:(hh + 1) * HEAD_DIM] = acc_ref[hh].astype(o_ref.dtype)
        return carry

    lax.fori_loop(0, seq // tq, q_body, 0)


def stick_breaking_attention(qkv, n_heads, *, tq_target=512, tk=V7X_MXU_DIM, heads_per_step=4):
    b, s, _ = qkv.shape
    tk = min(tk, s)
    tq = _pick_tile(s, tq_target, tk)
    hps = heads_per_step if n_heads % heads_per_step == 0 else 1
    nstep = n_heads // hps
    width = hps * HEAD_DIM
    u = (lax.broadcasted_iota(jnp.int32, (tk, tk), 0)
         > lax.broadcasted_iota(jnp.int32, (tk, tk), 1)).astype(BF16)
    kern = functools.partial(_sb_attn_kernel, tq=tq, tk=tk, heads=hps)
    return pl.pallas_call(
        kern,
        grid=(b, nstep),
        in_specs=[
            pl.BlockSpec((None, s, width), lambda bi, h: (bi, 0, h)),
            pl.BlockSpec((None, s, width), lambda bi, h: (bi, 0, nstep + h)),
            pl.BlockSpec((None, s, width), lambda bi, h: (bi, 0, 2 * nstep + h)),
            pl.BlockSpec((tk, tk), lambda bi, h: (0, 0)),
        ],
        out_specs=pl.BlockSpec((None, s, width), lambda bi, h: (bi, 0, h)),
        out_shape=jax.ShapeDtypeStruct((b, s, n_heads * HEAD_DIM), BF16),
        scratch_shapes=[
            pltpu.VMEM((hps, tq, HEAD_DIM), F32),
            pltpu.VMEM((hps, tq, V7X_LANES), F32),
            pltpu.VMEM((hps, tq, tq), F32),
            pltpu.VMEM((hps, tq, tq), F32),
            pltpu.VMEM((hps, tq, tq), BF16),
            pltpu.VMEM((hps, tq // tk, tq, tk), F32),
            pltpu.VMEM((hps, tq, tq), BF16),
        ],
        compiler_params=_params("parallel", "parallel"),
        name="stick_breaking_attention",
    )(qkv, qkv, qkv, u)


def _fox_attn_kernel(q_ref, k_ref, v_ref, ccol_ref, crow_ref, o_ref,
                     acc_ref, m_ref, l_ref, bt_ref, *, tq, group):
    seq = k_ref.shape[0]
    kvh = pl.program_id(1)
    reps = tq // V7X_LANES
    heads_per_dot = 2 if group % 2 == 0 else 1

    def head_stack(q0, k0, g0, ng, masked):
        q = jnp.concatenate(
            [q_ref[pl.ds(q0, tq), g * HEAD_DIM:(g + 1) * HEAD_DIM] for g in range(g0, g0 + ng)],
            axis=0)
        s_val = lax.dot_general(q, k_ref[pl.ds(k0, tq), :],
                                (((1,), (1,)), ((), ())), preferred_element_type=F32)
        p_parts = []
        for g in range(g0, g0 + ng):
            cs = crow_ref[g:g + 1, pl.ds(k0, tq)] * LOG2_E
            for r in range(tq // ATTN_ROW_CHUNK):
                rows = slice(r * ATTN_ROW_CHUNK, (r + 1) * ATTN_ROW_CHUNK)
                srows = slice((g - g0) * tq + rows.start, (g - g0) * tq + rows.stop)
                grows = slice(g * tq + rows.start, g * tq + rows.stop)
                u = s_val[srows, :] - cs
                if masked:
                    shape = (ATTN_ROW_CHUNK, tq)
                    causal = (lax.broadcasted_iota(jnp.int32, shape, 1)
                              <= rows.start + lax.broadcasted_iota(jnp.int32, shape, 0))
                    u = jnp.where(causal, u, MASK_VALUE)
                bt = bt_ref[grows, :]
                m_old = m_ref[grows, :]
                m_new = jnp.maximum(m_old, jnp.max(u, axis=-1, keepdims=True) + bt)
                alpha = jnp.exp2(m_old - m_new)
                p = jnp.exp2(u - jnp.tile(m_new - bt, (1, reps)))
                l_ref[grows, :] = alpha * l_ref[grows, :] + jnp.sum(p, axis=-1, keepdims=True)
                m_ref[grows, :] = m_new
                acc_ref[grows, :] = alpha * acc_ref[grows, :]
                p_parts.append(p.astype(BF16))
        trows = slice(g0 * tq, (g0 + ng) * tq)
        acc_ref[trows, :] += jnp.dot(jnp.concatenate(p_parts, axis=0), v_ref[pl.ds(k0, tq), :],
                                     preferred_element_type=F32)

    def one_block(q0, kb, masked):
        k0 = pl.multiple_of(kb * tq, tq)
        for g0 in range(0, group, heads_per_dot):
            head_stack(q0, k0, g0, heads_per_dot, masked)

    def q_body(qi, carry):
        q0 = pl.multiple_of(qi * tq, tq)
        cblk = ccol_ref[pl.ds(q0, tq), :]
        lane = lax.broadcasted_iota(jnp.int32, cblk.shape, 1)
        for g in range(group):
            c_t = jnp.sum(jnp.where(lane == kvh * group + g, cblk, 0.0), axis=-1, keepdims=True)
            bt_ref[g * tq:(g + 1) * tq, :] = jnp.broadcast_to(c_t * LOG2_E, (tq, V7X_LANES))
        acc_ref[...] = jnp.zeros_like(acc_ref)
        l_ref[...] = jnp.zeros_like(l_ref)
        m_ref[...] = jnp.full_like(m_ref, MASK_VALUE)
        one_block(q0, qi, True)

        def kb_body(kb, c):
            one_block(q0, kb, False)
            return c

        lax.fori_loop(0, qi, kb_body, 0)
        out = acc_ref[...] / l_ref[...]
        for g in range(group):
            o_ref[pl.ds(q0, tq), g * HEAD_DIM:(g + 1) * HEAD_DIM] = (
                out[g * tq:(g + 1) * tq, :].astype(o_ref.dtype))
        return carry

    lax.fori_loop(0, seq // tq, q_body, 0)


def forgetting_attention(q, kv, c_col, c_row, n_kv_heads, *, tq_target=512):
    b, s, dq = q.shape
    assert HEAD_DIM == V7X_LANES
    group = dq // HEAD_DIM // n_kv_heads
    tq = _pick_tile(s, tq_target, V7X_LANES)
    cw = c_col.shape[2]
    kern = functools.partial(_fox_attn_kernel, tq=tq, group=group)
    stat = pltpu.VMEM((group * tq, V7X_LANES), F32)
    return pl.pallas_call(
        kern,
        grid=(b, n_kv_heads),
        in_specs=[
            pl.BlockSpec((None, s, group * HEAD_DIM), lambda bi, h: (bi, 0, h)),
            pl.BlockSpec((None, s, HEAD_DIM), lambda bi, h: (bi, 0, h)),
            pl.BlockSpec((None, s, HEAD_DIM), lambda bi, h: (bi, 0, n_kv_heads + h)),
            pl.BlockSpec((None, s, cw), lambda bi, h: (bi, 0, 0)),
            pl.BlockSpec((None, None, group, s), lambda bi, h: (bi, h, 0, 0)),
        ],
        out_specs=pl.BlockSpec((None, s, group * HEAD_DIM), lambda bi, h: (bi, 0, h)),
        out_shape=jax.ShapeDtypeStruct((b, s, dq), BF16),
        scratch_shapes=[stat, stat, stat, stat],
        compiler_params=_params("parallel", "parallel"),
        name="forgetting_attention",
    )(q, kv, kv, c_col, c_row)


def _rms_norm_kernel(h_ref, g_ref, o_ref):
    x = h_ref[...]
    ms = jnp.mean(x * x, axis=-1, keepdims=True)
    o_ref[...] = x * lax.rsqrt(ms + RMS_EPS) * g_ref[...]


def rms_norm(h, gain, *, tm_target=256):
    m, d = h.shape
    tm = _pick_tile(m, tm_target, V7X_SUBLANES)
    return pl.pallas_call(
        _rms_norm_kernel,
        grid=(m // tm,),
        in_specs=[pl.BlockSpec((tm, d), lambda i: (i, 0)),
                  pl.BlockSpec((1, d), lambda i: (0, 0))],
        out_specs=pl.BlockSpec((tm, d), lambda i: (i, 0)),
        out_shape=jax.ShapeDtypeStruct((m, d), F32),
        compiler_params=_params("parallel"),
        name="final_rms_norm",
    )(h, gain.reshape(1, d))


def _conv_glu_ffn(h, gain, w_up, w_conv, b_conv, w_down, seq):
    (xn,) = rms_norm_cast(h, [gain])
    act = ffn_up(xn, w_up, w_conv, b_conv, seq)
    return matmul_residual(act, w_down.astype(BF16), h, tm_target=512, tn_target=512)


def kernel(x, a_attn_norm, a_w_qkv, a_w_o, a_ffn_norm, a_w_up, a_w_conv, a_b_conv, a_w_down,
           kv_norm, w_kv, w_f, b_f,
           b_attn_norm, b_w_q, b_w_o, b_ffn_norm, b_w_up, b_w_conv, b_b_conv, b_w_down,
           final_norm):
    bsz, seq, d = x.shape
    m = bsz * seq
    n_heads = d // HEAD_DIM
    n_kv_heads = w_kv.shape[1] // (2 * HEAD_DIM)
    group = n_heads // n_kv_heads
    n_gates = w_f.shape[1]
    assert n_gates == n_heads and n_gates <= V7X_LANES

    h = x.reshape(m, d)

    for i in range(a_w_qkv.shape[0]):
        (xn,) = rms_norm_cast(h, [a_attn_norm[i]])
        qkv = matmul(xn, a_w_qkv[i].astype(BF16), scaled_cols=d, scale=SB_Q_SCALE)
        o = stick_breaking_attention(qkv.reshape(bsz, seq, 3 * d), n_heads)
        h = matmul_residual(o.reshape(m, d), a_w_o[i].astype(BF16), h, tm_target=1024, tn_target=1024)
        h = _conv_glu_ffn(h, a_ffn_norm[i], a_w_up[i], a_w_conv[i], a_b_conv[i], a_w_down[i], seq)

    kv = c_col = c_row = None
    for j in range(b_w_q.shape[0]):
        if j == 0:
            xn_kv, xn = rms_norm_cast(h, [kv_norm, b_attn_norm[j]])
            wf_pad = jnp.pad(w_f, ((0, 0), (0, V7X_LANES - n_gates))).astype(BF16)
            bf_pad = jnp.pad(b_f, (0, V7X_LANES - n_gates)).reshape(1, V7X_LANES)
            kv, log_f = kv_proj(xn_kv, w_kv.astype(BF16), wf_pad, bf_pad)
            kv = kv.reshape(bsz, seq, 2 * n_kv_heads * HEAD_DIM)
            c_col = cumsum_positions(log_f.reshape(bsz, seq, V7X_LANES))
            c_row = (c_col[:, :, :n_gates].transpose(0, 2, 1)
                     .reshape(bsz, n_kv_heads, group, seq))
        else:
            (xn,) = rms_norm_cast(h, [b_attn_norm[j]])
        q = matmul(xn, b_w_q[j].astype(BF16), scaled_cols=d, scale=FOX_Q_SCALE)
        o = forgetting_attention(q.reshape(bsz, seq, d), kv, c_col, c_row, n_kv_heads)
        h = matmul_residual(o.reshape(m, d), b_w_o[j].astype(BF16), h, tm_target=1024, tn_target=1024)
        h = _conv_glu_ffn(h, b_ffn_norm[j], b_w_up[j], b_w_conv[j], b_b_conv[j], b_w_down[j], seq)

    return rms_norm(h, final_norm).reshape(bsz, seq, d)
```

```python
import functools
import math

import jax
import jax.numpy as jnp
from jax import lax
from jax.experimental import pallas as pl
from jax.experimental.pallas import tpu as pltpu

HEAD_DIM = 128
CONV_WIDTH = 3
RMS_EPS = 1e-6

V7X_LANES = 128
V7X_SUBLANES = 8
V7X_MXU_DIM = 256
V7X_VMEM_LIMIT_BYTES = 56 * 1024 * 1024

F32 = jnp.float32
BF16 = jnp.bfloat16

NORM_ROW_CHUNK = 32
ATTN_ROW_CHUNK = 32
MASK_VALUE = -1e30
LOG2_E = 1.4426950408889634
F32_SIGN_BIT = 0x80000000
FOX_Q_SCALE = HEAD_DIM ** -0.5 * LOG2_E
SB_Q_SCALE = -FOX_Q_SCALE


def _params(*semantics):
    return pltpu.CompilerParams(dimension_semantics=semantics,
                                vmem_limit_bytes=V7X_VMEM_LIMIT_BYTES)


def _pick_tile(n, target, quantum):
    if n <= target:
        return n
    t = (target // quantum) * quantum
    while t >= quantum:
        if n % t == 0:
            return t
        t -= quantum
    raise ValueError(f"no tile for {n}")


def _rms_norm_cast_kernel(h_ref, g_ref, *xn_refs):
    tm = h_ref.shape[0]

    def body(c, carry):
        r0 = pl.multiple_of(c * NORM_ROW_CHUNK, NORM_ROW_CHUNK)
        x = h_ref[pl.ds(r0, NORM_ROW_CHUNK), :]
        ms = jnp.mean(x * x, axis=-1, keepdims=True)
        y = x * lax.rsqrt(ms + RMS_EPS)
        for k, xn_ref in enumerate(xn_refs):
            xn_ref[pl.ds(r0, NORM_ROW_CHUNK), :] = (y * g_ref[k:k + 1, :]).astype(xn_ref.dtype)
        return carry

    lax.fori_loop(0, tm // NORM_ROW_CHUNK, body, 0, unroll=2)


def rms_norm_cast(h, gains, *, tm_target=512):
    m, d = h.shape
    ng = len(gains)
    tm = _pick_tile(m, tm_target, 2 * NORM_ROW_CHUNK)
    out = pl.pallas_call(
        _rms_norm_cast_kernel,
        grid=(m // tm,),
        in_specs=[pl.BlockSpec((tm, d), lambda i: (i, 0)),
                  pl.BlockSpec((ng, d), lambda i: (0, 0))],
        out_specs=[pl.BlockSpec((tm, d), lambda i: (i, 0))] * ng,
        out_shape=[jax.ShapeDtypeStruct((m, d), BF16)] * ng,
        compiler_params=_params("parallel"),
        name="rms_norm_cast",
    )(h, jnp.stack(gains))
    return list(out)


def _matmul_kernel(x_ref, w_ref, o_ref, *, scaled_tiles, scale):
    acc = jnp.dot(x_ref[...], w_ref[...], preferred_element_type=F32)
    if scaled_tiles:
        acc = acc * jnp.where(pl.program_id(1) < scaled_tiles, scale, 1.0)
    o_ref[...] = acc.astype(o_ref.dtype)


def matmul(x, w, *, scaled_cols=0, scale=1.0, tm_target=2048, tn_target=512):
    m, k = x.shape
    n = w.shape[1]
    tm = _pick_tile(m, tm_target, 2 * V7X_SUBLANES)
    tn = _pick_tile(n, tn_target, V7X_LANES)
    assert scaled_cols % tn == 0
    return pl.pallas_call(
        functools.partial(_matmul_kernel, scaled_tiles=scaled_cols // tn, scale=scale),
        grid=(m // tm, n // tn),
        in_specs=[
            pl.BlockSpec((tm, k), lambda i, j: (i, 0)),
            pl.BlockSpec((k, tn), lambda i, j: (0, j)),
        ],
        out_specs=pl.BlockSpec((tm, tn), lambda i, j: (i, j)),
        out_shape=jax.ShapeDtypeStruct((m, n), BF16),
        compiler_params=_params("parallel", "parallel"),
        name="matmul",
    )(x, w)


def _kv_proj_kernel(x_ref, w_ref, wf_ref, bf_ref, o_ref, lf_ref):
    @pl.when(pl.program_id(1) == 0)
    def _():
        pre = jnp.dot(x_ref[...], wf_ref[...], preferred_element_type=F32) + bf_ref[...]
        lf_ref[...] = -(jnp.maximum(-pre, 0.0) + jnp.log(1.0 + jnp.exp(-jnp.abs(pre))))

    o_ref[...] = jnp.dot(x_ref[...], w_ref[...], preferred_element_type=F32).astype(o_ref.dtype)


def kv_proj(x, w_kv, wf_pad, bf_pad, *, tm_target=1024, tn_target=1024):
    m, d = x.shape
    n = w_kv.shape[1]
    nf = wf_pad.shape[1]
    tm = _pick_tile(m, tm_target, 2 * V7X_SUBLANES)
    tn = _pick_tile(n, tn_target, V7X_LANES)
    return pl.pallas_call(
        _kv_proj_kernel,
        grid=(m // tm, n // tn),
        in_specs=[
            pl.BlockSpec((tm, d), lambda i, j: (i, 0)),
            pl.BlockSpec((d, tn), lambda i, j: (0, j)),
            pl.BlockSpec((d, nf), lambda i, j: (0, 0)),
            pl.BlockSpec((1, nf), lambda i, j: (0, 0)),
        ],
        out_specs=[
            pl.BlockSpec((tm, tn), lambda i, j: (i, j)),
            pl.BlockSpec((tm, nf), lambda i, j: (i, 0)),
        ],
        out_shape=[
            jax.ShapeDtypeStruct((m, n), BF16),
            jax.ShapeDtypeStruct((m, nf), F32),
        ],
        compiler_params=_params("parallel", "arbitrary"),
        name="kv_proj",
    )(x, w_kv, wf_pad, bf_pad)


def _split3(x):
    hi = x.astype(BF16)
    r1 = x - hi.astype(F32)
    mid = r1.astype(BF16)
    lo = (r1 - mid.astype(F32)).astype(BF16)
    return hi, mid, lo


def _cumsum_kernel(x_ref, tri_ref, c_ref, run_ref):
    blk = tri_ref.shape[0]
    nblk = x_ref.shape[0] // blk
    run_ref[...] = jnp.zeros_like(run_ref)

    def body(n, carry):
        r0 = pl.multiple_of(n * blk, blk)
        hi, mid, lo = _split3(x_ref[pl.ds(r0, blk), :])
        tri = tri_ref[...]
        incl = (jnp.dot(tri, hi, preferred_element_type=F32)
                + jnp.dot(tri, mid, preferred_element_type=F32)
                + jnp.dot(tri, lo, preferred_element_type=F32)
                + run_ref[0:1, :])
        c_ref[pl.ds(r0, blk), :] = incl
        run_ref[0:1, :] = incl[blk - 1:blk, :]
        return carry

    lax.fori_loop(0, nblk, body, 0)


def cumsum_positions(x):
    b, s, c = x.shape
    blk = V7X_LANES
    tri = (lax.broadcasted_iota(jnp.int32, (blk, blk), 1)
           <= lax.broadcasted_iota(jnp.int32, (blk, blk), 0)).astype(BF16)
    return pl.pallas_call(
        _cumsum_kernel,
        grid=(b,),
        in_specs=[
            pl.BlockSpec((None, s, c), lambda i: (i, 0, 0)),
            pl.BlockSpec((blk, blk), lambda i: (0, 0)),
        ],
        out_specs=pl.BlockSpec((None, s, c), lambda i: (i, 0, 0)),
        out_shape=jax.ShapeDtypeStruct((b, s, c), F32),
        scratch_shapes=[pltpu.VMEM((V7X_SUBLANES, c), F32)],
        compiler_params=_params("parallel"),
        name="cumsum_positions",
    )(x, tri)


def _matmul_residual_kernel(a_ref, w_ref, r_ref, o_ref):
    o_ref[...] = r_ref[...] + jnp.dot(a_ref[...], w_ref[...], preferred_element_type=F32)


def matmul_residual(a, w, res, *, tm_target, tn_target):
    m, k = a.shape
    n = w.shape[1]
    tm = _pick_tile(m, tm_target, 2 * V7X_SUBLANES)
    tn = _pick_tile(n, tn_target, V7X_LANES)
    return pl.pallas_call(
        _matmul_residual_kernel,
        grid=(m // tm, n // tn),
        in_specs=[
            pl.BlockSpec((tm, k), lambda i, j: (i, 0)),
            pl.BlockSpec((k, tn), lambda i, j: (0, j)),
            pl.BlockSpec((tm, tn), lambda i, j: (i, j)),
        ],
        out_specs=pl.BlockSpec((tm, tn), lambda i, j: (i, j)),
        out_shape=jax.ShapeDtypeStruct((m, n), F32),
        compiler_params=_params("parallel", "parallel"),
        name="matmul_residual",
    )(a, w, res)


def _ffn_up_kernel(xn_ref, wg_ref, wv_ref, cp_ref, o_ref, halo_ref, tail_ref, wb_ref,
                   *, tiles_per_seq, row_splits):
    i = pl.program_id(0)
    j = pl.program_id(1)
    tm, tn = o_ref.shape
    halo = V7X_SUBLANES
    mc = tm // row_splits

    seq_start = (i % tiles_per_seq) == 0

    @pl.when(seq_start)
    def _():
        halo_ref[...] = jnp.zeros_like(halo_ref)

    @pl.when(jnp.logical_not(seq_start))
    def _():
        halo_ref[...] = tail_ref[j]

    wb_ref[0] = wg_ref[...].astype(BF16)
    wb_ref[1] = wv_ref[...].astype(BF16)

    cp = cp_ref[j]
    prev_rows = halo_ref[...]
    for c in range(row_splits):
        rows = slice(c * mc, (c + 1) * mc)
        xc = xn_ref[rows, :]
        gate = jnp.dot(xc, wb_ref[0], preferred_element_type=F32)
        val = jnp.dot(xc, wb_ref[1], preferred_element_type=F32)
        ext = jnp.concatenate([prev_rows, gate], axis=0)
        g1 = pltpu.roll(ext, 1, 0)[halo:, :]
        g2 = pltpu.roll(ext, 2, 0)[halo:, :]
        conv = cp[3:4, :] + (cp[0:1, :] * g2 + cp[1:2, :] * g1 + cp[2:3, :] * gate)
        act = conv * (1.0 / (1.0 + jnp.exp(-conv))) * val
        o_ref[rows, :] = act.astype(o_ref.dtype)
        prev_rows = gate[mc - halo:mc, :]
    tail_ref[j] = prev_rows


def ffn_up(xn, w_up, w_conv, b_conv, seq, *, tm_target=2048, tn_target=256, chunk_rows=512):
    m, d = xn.shape
    f = w_up.shape[1] // 2
    tm = _pick_tile(seq, tm_target, 2 * V7X_SUBLANES)
    tn = _pick_tile(f, tn_target, V7X_LANES)
    nj = f // tn
    halo = V7X_SUBLANES
    row_splits = tm // chunk_rows if tm % chunk_rows == 0 else 1
    assert tm // row_splits >= halo and CONV_WIDTH - 1 <= halo and CONV_WIDTH + 1 <= V7X_SUBLANES
    conv_params = jnp.concatenate(
        [w_conv, b_conv.reshape(1, f),
         jnp.zeros((V7X_SUBLANES - CONV_WIDTH - 1, f), F32)], axis=0)
    conv_params = conv_params.reshape(V7X_SUBLANES, nj, tn).transpose(1, 0, 2)
    return pl.pallas_call(
        functools.partial(_ffn_up_kernel, tiles_per_seq=seq // tm, row_splits=row_splits),
        grid=(m // tm, nj),
        in_specs=[
            pl.BlockSpec((tm, d), lambda i, j: (i, 0)),
            pl.BlockSpec((d, tn), lambda i, j: (0, j)),
            pl.BlockSpec((d, tn), lambda i, j: (0, j + nj)),
            pl.BlockSpec((nj, V7X_SUBLANES, tn), lambda i, j: (0, 0, 0)),
        ],
        out_specs=pl.BlockSpec((tm, tn), lambda i, j: (i, j)),
        out_shape=jax.ShapeDtypeStruct((m, f), BF16),
        scratch_shapes=[
            pltpu.VMEM((halo, tn), F32),
            pltpu.VMEM((nj, halo, tn), F32),
            pltpu.VMEM((2, d, tn), BF16),
        ],
        compiler_params=_params("arbitrary", "arbitrary"),
        name="ffn_up",
    )(xn, w_up, w_up, conv_params)


def _sb_attn_kernel(q_ref, k_ref, v_ref, u_ref, o_ref,
                    acc_ref, run_ref, z_ref, own_ref, lsm_ref, incl_ref, w_ref,
                    *, tq, tk, heads):
    seq = q_ref.shape[0]
    nsub = tq // tk
    reps = tk // V7X_LANES
    nchunk = tq // ATTN_ROW_CHUNK

    def strict_mask(r):
        shape = (ATTN_ROW_CHUNK, tq)
        return (lax.broadcasted_iota(jnp.int32, shape, 1)
                < r * ATTN_ROW_CHUNK + lax.broadcasted_iota(jnp.int32, shape, 0))

    def one_block(q0, kb, masked):
        k0 = pl.multiple_of(kb * tq, tq)
        for hh in range(heads):
            cols_h = slice(hh * HEAD_DIM, (hh + 1) * HEAD_DIM)
            z_ref[hh] = lax.dot_general(q_ref[pl.ds(q0, tq), cols_h], k_ref[pl.ds(k0, tq), cols_h],
                                        (((1,), (1,)), ((), ())), preferred_element_type=F32)
            for r in range(nchunk):
                rows = slice(r * ATTN_ROW_CHUNK, (r + 1) * ATTN_ROW_CHUNK)
                nz2 = z_ref[hh, rows, :]
                neg_abs = lax.bitcast_convert_type(
                    lax.bitcast_convert_type(nz2, jnp.uint32) | jnp.uint32(F32_SIGN_BIT), F32)
                lsm = jnp.minimum(nz2, 0.0) - jnp.log2(1.0 + jnp.exp2(neg_abs))
                own_ref[hh, rows, :] = (lsm - nz2).astype(BF16)
                if masked:
                    lsm = jnp.where(strict_mask(r), lsm, 0.0)
                lsm_ref[hh, rows, :] = lsm.astype(BF16)
            for c in range(nsub):
                incl_ref[hh, c] = jnp.dot(lsm_ref[hh, :, c * tk:(c + 1) * tk], u_ref[...],
                                          preferred_element_type=F32)
            for r in range(nchunk):
                rows = slice(r * ATTN_ROW_CHUNK, (r + 1) * ATTN_ROW_CHUNK)
                run = run_ref[hh, rows, :]
                for c in reversed(range(nsub)):
                    cols = slice(c * tk, (c + 1) * tk)
                    excl = incl_ref[hh, c, rows, :]
                    e2 = excl + jnp.tile(run, (1, reps)) + own_ref[hh, rows, cols].astype(F32)
                    w = jnp.exp2(e2)
                    if masked:
                        w = jnp.where(strict_mask(r)[:, cols], w, 0.0)
                    w_ref[hh, rows, cols] = w.astype(BF16)
                    first = lsm_ref[hh, rows, c * tk:c * tk + V7X_LANES].astype(F32)
                    run = run + jnp.broadcast_to(excl[:, 0:1] + first[:, 0:1], run.shape)
                run_ref[hh, rows, :] = run
            acc_ref[hh] += jnp.dot(w_ref[hh], v_ref[pl.ds(k0, tq), cols_h],
                                   preferred_element_type=F32)

    def q_body(qi, carry):
        q0 = pl.multiple_of(qi * tq, tq)
        acc_ref[...] = jnp.zeros_like(acc_ref)
        run_ref[...] = jnp.zeros_like(run_ref)
        one_block(q0, qi, True)

        def kb_body(n, c):
            one_block(q0, qi - 1 - n, False)
            return c

        lax.fori_loop(0, qi, kb_body, 0)
        for hh in range(heads):
            o_ref[pl.ds(q0, tq), hh * HEAD_DIM:(hh + 1) * HEAD_DIM] = acc_ref[hh].astype(o_ref.dtype)
        return carry

    lax.fori_loop(0, seq // tq, q_body, 0)


def stick_breaking_attention(qkv, n_heads, *, tq_target=512, tk=V7X_MXU_DIM, heads_per_step=4):
    b, s, _ = qkv.shape
    tk = min(tk, s)
    tq = _pick_tile(s, tq_target, tk)
    hps = heads_per_step if n_heads % heads_per_step == 0 else 1
    nstep = n_heads // hps
    width = hps * HEAD_DIM
    u = (lax.broadcasted_iota(jnp.int32, (tk, tk), 0)
         > lax.broadcasted_iota(jnp.int32, (tk, tk), 1)).astype(BF16)
    kern = functools.partial(_sb_attn_kernel, tq=tq, tk=tk, heads=hps)
    return pl.pallas_call(
        kern,
        grid=(b, nstep),
        in_specs=[
            pl.BlockSpec((None, s, width), lambda bi, h: (bi, 0, h)),
            pl.BlockSpec((None, s, width), lambda bi, h: (bi, 0, nstep + h)),
            pl.BlockSpec((None, s, width), lambda bi, h: (bi, 0, 2 * nstep + h)),
            pl.BlockSpec((tk, tk), lambda bi, h: (0, 0)),
        ],
        out_specs=pl.BlockSpec((None, s, width), lambda bi, h: (bi, 0, h)),
        out_shape=jax.ShapeDtypeStruct((b, s, n_heads * HEAD_DIM), BF16),
        scratch_shapes=[
            pltpu.VMEM((hps, tq, HEAD_DIM), F32),
            pltpu.VMEM((hps, tq, V7X_LANES), F32),
            pltpu.VMEM((hps, tq, tq), F32),
            pltpu.VMEM((hps, tq, tq), BF16),
            pltpu.VMEM((hps, tq, tq), BF16),
            pltpu.VMEM((hps, tq // tk, tq, tk), F32),
            pltpu.VMEM((hps, tq, tq), BF16),
        ],
        compiler_params=_params("parallel", "parallel"),
        name="stick_breaking_attention",
    )(qkv, qkv, qkv, u)


def _fox_attn_kernel(q_ref, k_ref, v_ref, ccol_ref, crow_ref, o_ref,
                     acc_ref, m_ref, l_ref, bt_ref, *, tq, group):
    seq = k_ref.shape[0]
    kvh = pl.program_id(1)
    reps = tq // V7X_LANES
    heads_per_dot = 2 if group % 2 == 0 else 1

    def head_stack(q0, k0, g0, ng, masked):
        q = jnp.concatenate(
            [q_ref[pl.ds(q0, tq), g * HEAD_DIM:(g + 1) * HEAD_DIM] for g in range(g0, g0 + ng)],
            axis=0)
        s_val = lax.dot_general(q, k_ref[pl.ds(k0, tq), :],
                                (((1,), (1,)), ((), ())), preferred_element_type=F32)
        p_parts = []
        for g in range(g0, g0 + ng):
            cs = crow_ref[g:g + 1, pl.ds(k0, tq)] * LOG2_E
            for r in range(tq // ATTN_ROW_CHUNK):
                rows = slice(r * ATTN_ROW_CHUNK, (r + 1) * ATTN_ROW_CHUNK)
                srows = slice((g - g0) * tq + rows.start, (g - g0) * tq + rows.stop)
                grows = slice(g * tq + rows.start, g * tq + rows.stop)
                u = s_val[srows, :] - cs
                if masked:
                    shape = (ATTN_ROW_CHUNK, tq)
                    causal = (lax.broadcasted_iota(jnp.int32, shape, 1)
                              <= rows.start + lax.broadcasted_iota(jnp.int32, shape, 0))
                    u = jnp.where(causal, u, MASK_VALUE)
                bt = bt_ref[grows, :]
                m_old = m_ref[grows, :]
                m_new = jnp.maximum(m_old, jnp.max(u, axis=-1, keepdims=True) + bt)
                alpha = jnp.exp2(m_old - m_new)
                p = jnp.exp2(u - jnp.tile(m_new - bt, (1, reps)))
                l_ref[grows, :] = alpha * l_ref[grows, :] + jnp.sum(p, axis=-1, keepdims=True)
                m_ref[grows, :] = m_new
                acc_ref[grows, :] = alpha * acc_ref[grows, :]
                p_parts.append(p.astype(BF16))
        trows = slice(g0 * tq, (g0 + ng) * tq)
        acc_ref[trows, :] += jnp.dot(jnp.concatenate(p_parts, axis=0), v_ref[pl.ds(k0, tq), :],
                                     preferred_element_type=F32)

    def one_block(q0, kb, masked):
        k0 = pl.multiple_of(kb * tq, tq)
        for g0 in range(0, group, heads_per_dot):
            head_stack(q0, k0, g0, heads_per_dot, masked)

    def q_body(qi, carry):
        q0 = pl.multiple_of(qi * tq, tq)
        cblk = ccol_ref[pl.ds(q0, tq), :]
        lane = lax.broadcasted_iota(jnp.int32, cblk.shape, 1)
        for g in range(group):
            c_t = jnp.sum(jnp.where(lane == kvh * group + g, cblk, 0.0), axis=-1, keepdims=True)
            bt_ref[g * tq:(g + 1) * tq, :] = jnp.broadcast_to(c_t * LOG2_E, (tq, V7X_LANES))
        acc_ref[...] = jnp.zeros_like(acc_ref)
        l_ref[...] = jnp.zeros_like(l_ref)
        m_ref[...] = jnp.full_like(m_ref, MASK_VALUE)
        one_block(q0, qi, True)

        def kb_body(kb, c):
            one_block(q0, kb, False)
            return c

        lax.fori_loop(0, qi, kb_body, 0)
        out = acc_ref[...] / l_ref[...]
        for g in range(group):
            o_ref[pl.ds(q0, tq), g * HEAD_DIM:(g + 1) * HEAD_DIM] = (
                out[g * tq:(g + 1) * tq, :].astype(o_ref.dtype))
        return carry

    lax.fori_loop(0, seq // tq, q_body, 0)


def forgetting_attention(q, kv, c_col, c_row, n_kv_heads, *, tq_target=512):
    b, s, dq = q.shape
    assert HEAD_DIM == V7X_LANES
    group = dq // HEAD_DIM // n_kv_heads
    tq = _pick_tile(s, tq_target, V7X_LANES)
    cw = c_col.shape[2]
    kern = functools.partial(_fox_attn_kernel, tq=tq, group=group)
    stat = pltpu.VMEM((group * tq, V7X_LANES), F32)
    return pl.pallas_call(
        kern,
        grid=(b, n_kv_heads),
        in_specs=[
            pl.BlockSpec((None, s, group * HEAD_DIM), lambda bi, h: (bi, 0, h)),
            pl.BlockSpec((None, s, HEAD_DIM), lambda bi, h: (bi, 0, h)),
            pl.BlockSpec((None, s, HEAD_DIM), lambda bi, h: (bi, 0, n_kv_heads + h)),
            pl.BlockSpec((None, s, cw), lambda bi, h: (bi, 0, 0)),
            pl.BlockSpec((None, None, group, s), lambda bi, h: (bi, h, 0, 0)),
        ],
        out_specs=pl.BlockSpec((None, s, group * HEAD_DIM), lambda bi, h: (bi, 0, h)),
        out_shape=jax.ShapeDtypeStruct((b, s, dq), BF16),
        scratch_shapes=[stat, stat, stat, stat],
        compiler_params=_params("parallel", "parallel"),
        name="forgetting_attention",
    )(q, kv, kv, c_col, c_row)


def _rms_norm_kernel(h_ref, g_ref, o_ref):
    x = h_ref[...]
    ms = jnp.mean(x * x, axis=-1, keepdims=True)
    o_ref[...] = x * lax.rsqrt(ms + RMS_EPS) * g_ref[...]


def rms_norm(h, gain, *, tm_target=256):
    m, d = h.shape
    tm = _pick_tile(m, tm_target, V7X_SUBLANES)
    return pl.pallas_call(
        _rms_norm_kernel,
        grid=(m // tm,),
        in_specs=[pl.BlockSpec((tm, d), lambda i: (i, 0)),
                  pl.BlockSpec((1, d), lambda i: (0, 0))],
        out_specs=pl.BlockSpec((tm, d), lambda i: (i, 0)),
        out_shape=jax.ShapeDtypeStruct((m, d), F32),
        compiler_params=_params("parallel"),
        name="final_rms_norm",
    )(h, gain.reshape(1, d))


def _conv_glu_ffn(h, gain, w_up, w_conv, b_conv, w_down, seq):
    (xn,) = rms_norm_cast(h, [gain])
    act = ffn_up(xn, w_up, w_conv, b_conv, seq)
    return matmul_residual(act, w_down.astype(BF16), h, tm_target=512, tn_target=512)


def kernel(x, a_attn_norm, a_w_qkv, a_w_o, a_ffn_norm, a_w_up, a_w_conv, a_b_conv, a_w_down,
           kv_norm, w_kv, w_f, b_f,
           b_attn_norm, b_w_q, b_w_o, b_ffn_norm, b_w_up, b_w_conv, b_b_conv, b_w_down,
           final_norm):
    bsz, seq, d = x.shape
    m = bsz * seq
    n_heads = d // HEAD_DIM
    n_kv_heads = w_kv.shape[1] // (2 * HEAD_DIM)
    group = n_heads // n_kv_heads
    n_gates = w_f.shape[1]
    assert n_gates == n_heads and n_gates <= V7X_LANES

    h = x.reshape(m, d)

    for i in range(a_w_qkv.shape[0]):
        (xn,) = rms_norm_cast(h, [a_attn_norm[i]])
        qkv = matmul(xn, a_w_qkv[i].astype(BF16), scaled_cols=d, scale=SB_Q_SCALE)
        o = stick_breaking_attention(qkv.reshape(bsz, seq, 3 * d), n_heads)
        h = matmul_residual(o.reshape(m, d), a_w_o[i].astype(BF16), h, tm_target=1024, tn_target=1024)
        h = _conv_glu_ffn(h, a_ffn_norm[i], a_w_up[i], a_w_conv[i], a_b_conv[i], a_w_down[i], seq)

    kv = c_col = c_row = None
    for j in range(b_w_q.shape[0]):
        if j == 0:
            xn_kv, xn = rms_norm_cast(h, [kv_norm, b_attn_norm[j]])
            wf_pad = jnp.pad(w_f, ((0, 0), (0, V7X_LANES - n_gates))).astype(BF16)
            bf_pad = jnp.pad(b_f, (0, V7X_LANES - n_gates)).reshape(1, V7X_LANES)
            kv, log_f = kv_proj(xn_kv, w_kv.astype(BF16), wf_pad, bf_pad)
            kv = kv.reshape(bsz, seq, 2 * n_kv_heads * HEAD_DIM)
            c_col = cumsum_positions(log_f.reshape(bsz, seq, V7X_LANES))
            c_row = (c_col[:, :, :n_gates].transpose(0, 2, 1)
                     .reshape(bsz, n_kv_heads, group, seq))
        else:
            (xn,) = rms_norm_cast(h, [b_attn_norm[j]])
        q = matmul(xn, b_w_q[j].astype(BF16), scaled_cols=d, scale=FOX_Q_SCALE)
        o = forgetting_attention(q.reshape(bsz, seq, d), kv, c_col, c_row, n_kv_heads)
        h = matmul_residual(o.reshape(m, d), b_w_o[j].astype(BF16), h, tm_target=1024, tn_target=1024)
        h = _conv_glu_ffn(h, b_ffn_norm[j], b_w_up[j], b_w_conv[j], b_b_conv[j], b_w_down[j], seq)

    return rms_norm(h, final_norm).reshape(bsz, seq, d)
```

```python
import functools
import math

import jax
import jax.numpy as jnp
from jax import lax
from jax.experimental import pallas as pl
from jax.experimental.pallas import tpu as pltpu

HEAD_DIM = 128
CONV_WIDTH = 3
RMS_EPS = 1e-6

V7X_LANES = 128
V7X_SUBLANES = 8
V7X_MXU_DIM = 256
V7X_VMEM_LIMIT_BYTES = 56 * 1024 * 1024

F32 = jnp.float32
BF16 = jnp.bfloat16

NORM_ROW_CHUNK = 32
ATTN_ROW_CHUNK = 32
MASK_VALUE = -1e30
LOG2_E = 1.4426950408889634
F32_SIGN_BIT = 0x80000000
FOX_Q_SCALE = HEAD_DIM ** -0.5 * LOG2_E
SB_Q_SCALE = -FOX_Q_SCALE


def _params(*semantics):
    return pltpu.CompilerParams(dimension_semantics=semantics,
                                vmem_limit_bytes=V7X_VMEM_LIMIT_BYTES)


def _pick_tile(n, target, quantum):
    if n <= target:
        return n
    t = (target // quantum) * quantum
    while t >= quantum:
        if n % t == 0:
            return t
        t -= quantum
    raise ValueError(f"no tile for {n}")


def _rms_norm_cast_kernel(h_ref, g_ref, *xn_refs):
    tm = h_ref.shape[0]

    def body(c, carry):
        r0 = pl.multiple_of(c * NORM_ROW_CHUNK, NORM_ROW_CHUNK)
        x = h_ref[pl.ds(r0, NORM_ROW_CHUNK), :]
        ms = jnp.mean(x * x, axis=-1, keepdims=True)
        y = x * lax.rsqrt(ms + RMS_EPS)
        for k, xn_ref in enumerate(xn_refs):
            xn_ref[pl.ds(r0, NORM_ROW_CHUNK), :] = (y * g_ref[k:k + 1, :]).astype(xn_ref.dtype)
        return carry

    lax.fori_loop(0, tm // NORM_ROW_CHUNK, body, 0, unroll=2)


def rms_norm_cast(h, gains, *, tm_target=512):
    m, d = h.shape
    ng = len(gains)
    tm = _pick_tile(m, tm_target, 2 * NORM_ROW_CHUNK)
    out = pl.pallas_call(
        _rms_norm_cast_kernel,
        grid=(m // tm,),
        in_specs=[pl.BlockSpec((tm, d), lambda i: (i, 0)),
                  pl.BlockSpec((ng, d), lambda i: (0, 0))],
        out_specs=[pl.BlockSpec((tm, d), lambda i: (i, 0))] * ng,
        out_shape=[jax.ShapeDtypeStruct((m, d), BF16)] * ng,
        compiler_params=_params("parallel"),
        name="rms_norm_cast",
    )(h, jnp.stack(gains))
    return list(out)


def _matmul_kernel(x_ref, w_ref, o_ref, *, scaled_tiles, scale):
    acc = jnp.dot(x_ref[...], w_ref[...], preferred_element_type=F32)
    if scaled_tiles:
        acc = acc * jnp.where(pl.program_id(1) < scaled_tiles, scale, 1.0)
    o_ref[...] = acc.astype(o_ref.dtype)


def matmul(x, w, *, scaled_cols=0, scale=1.0, tm_target=2048, tn_target=512):
    m, k = x.shape
    n = w.shape[1]
    tm = _pick_tile(m, tm_target, 2 * V7X_SUBLANES)
    tn = _pick_tile(n, tn_target, V7X_LANES)
    assert scaled_cols % tn == 0
    return pl.pallas_call(
        functools.partial(_matmul_kernel, scaled_tiles=scaled_cols // tn, scale=scale),
        grid=(m // tm, n // tn),
        in_specs=[
            pl.BlockSpec((tm, k), lambda i, j: (i, 0)),
            pl.BlockSpec((k, tn), lambda i, j: (0, j)),
        ],
        out_specs=pl.BlockSpec((tm, tn), lambda i, j: (i, j)),
        out_shape=jax.ShapeDtypeStruct((m, n), BF16),
        compiler_params=_params("parallel", "parallel"),
        name="matmul",
    )(x, w)


def _kv_proj_kernel(x_ref, w_ref, wf_ref, bf_ref, o_ref, lf_ref):
    @pl.when(pl.program_id(1) == 0)
    def _():
        pre = jnp.dot(x_ref[...], wf_ref[...], preferred_element_type=F32) + bf_ref[...]
        lf_ref[...] = -(jnp.maximum(-pre, 0.0) + jnp.log(1.0 + jnp.exp(-jnp.abs(pre))))

    o_ref[...] = jnp.dot(x_ref[...], w_ref[...], preferred_element_type=F32).astype(o_ref.dtype)


def kv_proj(x, w_kv, wf_pad, bf_pad, *, tm_target=1024, tn_target=1024):
    m, d = x.shape
    n = w_kv.shape[1]
    nf = wf_pad.shape[1]
    tm = _pick_tile(m, tm_target, 2 * V7X_SUBLANES)
    tn = _pick_tile(n, tn_target, V7X_LANES)
    return pl.pallas_call(
        _kv_proj_kernel,
        grid=(m // tm, n // tn),
        in_specs=[
            pl.BlockSpec((tm, d), lambda i, j: (i, 0)),
            pl.BlockSpec((d, tn), lambda i, j: (0, j)),
            pl.BlockSpec((d, nf), lambda i, j: (0, 0)),
            pl.BlockSpec((1, nf), lambda i, j: (0, 0)),
        ],
        out_specs=[
            pl.BlockSpec((tm, tn), lambda i, j: (i, j)),
            pl.BlockSpec((tm, nf), lambda i, j: (i, 0)),
        ],
        out_shape=[
            jax.ShapeDtypeStruct((m, n), BF16),
            jax.ShapeDtypeStruct((m, nf), F32),
        ],
        compiler_params=_params("parallel", "arbitrary"),
        name="kv_proj",
    )(x, w_kv, wf_pad, bf_pad)


def _split3(x):
    hi = x.astype(BF16)
    r1 = x - hi.astype(F32)
    mid = r1.astype(BF16)
    lo = (r1 - mid.astype(F32)).astype(BF16)
    return hi, mid, lo


def _cumsum_kernel(x_ref, tri_ref, c_ref, run_ref):
    blk = tri_ref.shape[0]
    nblk = x_ref.shape[0] // blk
    run_ref[...] = jnp.zeros_like(run_ref)

    def body(n, carry):
        r0 = pl.multiple_of(n * blk, blk)
        hi, mid, lo = _split3(x_ref[pl.ds(r0, blk), :])
        tri = tri_ref[...]
        incl = (jnp.dot(tri, hi, preferred_element_type=F32)
                + jnp.dot(tri, mid, preferred_element_type=F32)
                + jnp.dot(tri, lo, preferred_element_type=F32)
                + run_ref[0:1, :])
        c_ref[pl.ds(r0, blk), :] = incl
        run_ref[0:1, :] = incl[blk - 1:blk, :]
        return carry

    lax.fori_loop(0, nblk, body, 0)


def cumsum_positions(x):
    b, s, c = x.shape
    blk = V7X_LANES
    tri = (lax.broadcasted_iota(jnp.int32, (blk, blk), 1)
           <= lax.broadcasted_iota(jnp.int32, (blk, blk), 0)).astype(BF16)
    return pl.pallas_call(
        _cumsum_kernel,
        grid=(b,),
        in_specs=[
            pl.BlockSpec((None, s, c), lambda i: (i, 0, 0)),
            pl.BlockSpec((blk, blk), lambda i: (0, 0)),
        ],
        out_specs=pl.BlockSpec((None, s, c), lambda i: (i, 0, 0)),
        out_shape=jax.ShapeDtypeStruct((b, s, c), F32),
        scratch_shapes=[pltpu.VMEM((V7X_SUBLANES, c), F32)],
        compiler_params=_params("parallel"),
        name="cumsum_positions",
    )(x, tri)


def _matmul_residual_kernel(a_ref, w_ref, r_ref, o_ref):
    o_ref[...] = r_ref[...] + jnp.dot(a_ref[...], w_ref[...], preferred_element_type=F32)


def matmul_residual(a, w, res, *, tm_target, tn_target):
    m, k = a.shape
    n = w.shape[1]
    tm = _pick_tile(m, tm_target, 2 * V7X_SUBLANES)
    tn = _pick_tile(n, tn_target, V7X_LANES)
    return pl.pallas_call(
        _matmul_residual_kernel,
        grid=(m // tm, n // tn),
        in_specs=[
            pl.BlockSpec((tm, k), lambda i, j: (i, 0)),
            pl.BlockSpec((k, tn), lambda i, j: (0, j)),
            pl.BlockSpec((tm, tn), lambda i, j: (i, j)),
        ],
        out_specs=pl.BlockSpec((tm, tn), lambda i, j: (i, j)),
        out_shape=jax.ShapeDtypeStruct((m, n), F32),
        compiler_params=_params("parallel", "parallel"),
        name="matmul_residual",
    )(a, w, res)


def _ffn_up_kernel(xn_ref, wg_ref, wv_ref, cp_ref, o_ref, halo_ref, tail_ref, wb_ref,
                   *, tiles_per_seq, row_splits):
    i = pl.program_id(0)
    j = pl.program_id(1)
    tm, tn = o_ref.shape
    halo = V7X_SUBLANES
    mc = tm // row_splits

    seq_start = (i % tiles_per_seq) == 0

    @pl.when(seq_start)
    def _():
        halo_ref[...] = jnp.zeros_like(halo_ref)

    @pl.when(jnp.logical_not(seq_start))
    def _():
        halo_ref[...] = tail_ref[j]

    wb_ref[0] = wg_ref[...].astype(BF16)
    wb_ref[1] = wv_ref[...].astype(BF16)

    cp = cp_ref[j]
    prev_rows = halo_ref[...]
    for c in range(row_splits):
        rows = slice(c * mc, (c + 1) * mc)
        xc = xn_ref[rows, :]
        gate = jnp.dot(xc, wb_ref[0], preferred_element_type=F32)
        val = jnp.dot(xc, wb_ref[1], preferred_element_type=F32)
        ext = jnp.concatenate([prev_rows, gate], axis=0)
        g1 = pltpu.roll(ext, 1, 0)[halo:, :]
        g2 = pltpu.roll(ext, 2, 0)[halo:, :]
        conv = cp[3:4, :] + (cp[0:1, :] * g2 + cp[1:2, :] * g1 + cp[2:3, :] * gate)
        act = conv * (1.0 / (1.0 + jnp.exp(-conv))) * val
        o_ref[rows, :] = act.astype(o_ref.dtype)
        prev_rows = gate[mc - halo:mc, :]
    tail_ref[j] = prev_rows


def ffn_up(xn, w_up, w_conv, b_conv, seq, *, tm_target=2048, tn_target=256, chunk_rows=512):
    m, d = xn.shape
    f = w_up.shape[1] // 2
    tm = _pick_tile(seq, tm_target, 2 * V7X_SUBLANES)
    tn = _pick_tile(f, tn_target, V7X_LANES)
    nj = f // tn
    halo = V7X_SUBLANES
    row_splits = tm // chunk_rows if tm % chunk_rows == 0 else 1
    assert tm // row_splits >= halo and CONV_WIDTH - 1 <= halo and CONV_WIDTH + 1 <= V7X_SUBLANES
    conv_params = jnp.concatenate(
        [w_conv, b_conv.reshape(1, f),
         jnp.zeros((V7X_SUBLANES - CONV_WIDTH - 1, f), F32)], axis=0)
    conv_params = conv_params.reshape(V7X_SUBLANES, nj, tn).transpose(1, 0, 2)
    return pl.pallas_call(
        functools.partial(_ffn_up_kernel, tiles_per_seq=seq // tm, row_splits=row_splits),
        grid=(m // tm, nj),
        in_specs=[
            pl.BlockSpec((tm, d), lambda i, j: (i, 0)),
            pl.BlockSpec((d, tn), lambda i, j: (0, j)),
            pl.BlockSpec((d, tn), lambda i, j: (0, j + nj)),
            pl.BlockSpec((nj, V7X_SUBLANES, tn), lambda i, j: (0, 0, 0)),
        ],
        out_specs=pl.BlockSpec((tm, tn), lambda i, j: (i, j)),
        out_shape=jax.ShapeDtypeStruct((m, f), BF16),
        scratch_shapes=[
            pltpu.VMEM((halo, tn), F32),
            pltpu.VMEM((nj, halo, tn), F32),
            pltpu.VMEM((2, d, tn), BF16),
        ],
        compiler_params=_params("arbitrary", "arbitrary"),
        name="ffn_up",
    )(xn, w_up, w_up, conv_params)


def _sb_attn_kernel(q_ref, k_ref, v_ref, u_ref, o_ref,
                    acc_ref, run_ref, own_ref, lsm_ref, incl_ref, w_ref,
                    *, tq, tk, heads):
    seq = q_ref.shape[0]
    nsub = tq // tk
    reps = tk // V7X_LANES
    nchunk = tq // ATTN_ROW_CHUNK

    def strict_mask(r):
        shape = (ATTN_ROW_CHUNK, tq)
        return (lax.broadcasted_iota(jnp.int32, shape, 1)
                < r * ATTN_ROW_CHUNK + lax.broadcasted_iota(jnp.int32, shape, 0))

    def one_block(q0, kb, masked):
        k0 = pl.multiple_of(kb * tq, tq)
        for hh in range(heads):
            cols_h = slice(hh * HEAD_DIM, (hh + 1) * HEAD_DIM)
            nz2_all = lax.dot_general(q_ref[pl.ds(q0, tq), cols_h], k_ref[pl.ds(k0, tq), cols_h],
                                      (((1,), (1,)), ((), ())), preferred_element_type=F32)
            for r in range(nchunk):
                rows = slice(r * ATTN_ROW_CHUNK, (r + 1) * ATTN_ROW_CHUNK)
                nz2 = nz2_all[rows, :]
                neg_abs = lax.bitcast_convert_type(
                    lax.bitcast_convert_type(nz2, jnp.uint32) | jnp.uint32(F32_SIGN_BIT), F32)
                lsm = jnp.minimum(nz2, 0.0) - jnp.log2(1.0 + jnp.exp2(neg_abs))
                own_ref[hh, rows, :] = lsm - nz2
                if masked:
                    lsm = jnp.where(strict_mask(r), lsm, 0.0)
                lsm_ref[hh, rows, :] = lsm.astype(BF16)
            for c in range(nsub):
                incl_ref[hh, c] = jnp.dot(lsm_ref[hh, :, c * tk:(c + 1) * tk], u_ref[...],
                                          preferred_element_type=F32)
            for r in range(nchunk):
                rows = slice(r * ATTN_ROW_CHUNK, (r + 1) * ATTN_ROW_CHUNK)
                run = run_ref[hh, rows, :]
                for c in reversed(range(nsub)):
                    cols = slice(c * tk, (c + 1) * tk)
                    excl = incl_ref[hh, c, rows, :]
                    e2 = excl + jnp.tile(run, (1, reps)) + own_ref[hh, rows, cols]
                    w = jnp.exp2(e2)
                    if masked:
                        w = jnp.where(strict_mask(r)[:, cols], w, 0.0)
                    w_ref[hh, rows, cols] = w.astype(BF16)
                    first = lsm_ref[hh, rows, c * tk:c * tk + V7X_LANES].astype(F32)
                    run = run + jnp.broadcast_to(excl[:, 0:1] + first[:, 0:1], run.shape)
                run_ref[hh, rows, :] = run
            acc_ref[hh] += jnp.dot(w_ref[hh], v_ref[pl.ds(k0, tq), cols_h],
                                   preferred_element_type=F32)

    def q_body(qi, carry):
        q0 = pl.multiple_of(qi * tq, tq)
        acc_ref[...] = jnp.zeros_like(acc_ref)
        run_ref[...] = jnp.zeros_like(run_ref)
        one_block(q0, qi, True)

        def kb_body(n, c):
            one_block(q0, qi - 1 - n, False)
            return c

        lax.fori_loop(0, qi, kb_body, 0)
        for hh in range(heads):
            o_ref[pl.ds(q0, tq), hh * HEAD_DIM:(hh + 1) * HEAD_DIM] = acc_ref[hh].astype(o_ref.dtype)
        return carry

    lax.fori_loop(0, seq // tq, q_body, 0)


def stick_breaking_attention(qkv, n_heads, *, tq_target=512, tk=V7X_MXU_DIM, heads_per_step=4):
    b, s, _ = qkv.shape
    tk = min(tk, s)
    tq = _pick_tile(s, tq_target, tk)
    hps = heads_per_step if n_heads % heads_per_step == 0 else 1
    nstep = n_heads // hps
    width = hps * HEAD_DIM
    u = (lax.broadcasted_iota(jnp.int32, (tk, tk), 0)
         > lax.broadcasted_iota(jnp.int32, (tk, tk), 1)).astype(BF16)
    kern = functools.partial(_sb_attn_kernel, tq=tq, tk=tk, heads=hps)
    return pl.pallas_call(
        kern,
        grid=(b, nstep),
        in_specs=[
            pl.BlockSpec((None, s, width), lambda bi, h: (bi, 0, h)),
            pl.BlockSpec((None, s, width), lambda bi, h: (bi, 0, nstep + h)),
            pl.BlockSpec((None, s, width), lambda bi, h: (bi, 0, 2 * nstep + h)),
            pl.BlockSpec((tk, tk), lambda bi, h: (0, 0)),
        ],
        out_specs=pl.BlockSpec((None, s, width), lambda bi, h: (bi, 0, h)),
        out_shape=jax.ShapeDtypeStruct((b, s, n_heads * HEAD_DIM), BF16),
        scratch_shapes=[
            pltpu.VMEM((hps, tq, HEAD_DIM), F32),
            pltpu.VMEM((hps, tq, V7X_LANES), F32),
            pltpu.VMEM((hps, tq, tq), F32),
            pltpu.VMEM((hps, tq, tq), BF16),
            pltpu.VMEM((hps, tq // tk, tq, tk), F32),
            pltpu.VMEM((hps, tq, tq), BF16),
        ],
        compiler_params=_params("parallel", "parallel"),
        name="stick_breaking_attention",
    )(qkv, qkv, qkv, u)


def _fox_attn_kernel(q_ref, k_ref, v_ref, ccol_ref, crow_ref, o_ref,
                     acc_ref, m_ref, l_ref, bt_ref, *, tq, group):
    seq = k_ref.shape[0]
    kvh = pl.program_id(1)
    reps = tq // V7X_LANES
    heads_per_dot = 2 if group % 2 == 0 else 1

    def head_stack(q0, k0, g0, ng, masked):
        q = jnp.concatenate(
            [q_ref[pl.ds(q0, tq), g * HEAD_DIM:(g + 1) * HEAD_DIM] for g in range(g0, g0 + ng)],
            axis=0)
        s_val = lax.dot_general(q, k_ref[pl.ds(k0, tq), :],
                                (((1,), (1,)), ((), ())), preferred_element_type=F32)
        p_parts = []
        for g in range(g0, g0 + ng):
            cs = crow_ref[g:g + 1, pl.ds(k0, tq)] * LOG2_E
            for r in range(tq // ATTN_ROW_CHUNK):
                rows = slice(r * ATTN_ROW_CHUNK, (r + 1) * ATTN_ROW_CHUNK)
                srows = slice((g - g0) * tq + rows.start, (g - g0) * tq + rows.stop)
                grows = slice(g * tq + rows.start, g * tq + rows.stop)
                u = s_val[srows, :] - cs
                if masked:
                    shape = (ATTN_ROW_CHUNK, tq)
                    causal = (lax.broadcasted_iota(jnp.int32, shape, 1)
                              <= rows.start + lax.broadcasted_iota(jnp.int32, shape, 0))
                    u = jnp.where(causal, u, MASK_VALUE)
                bt = bt_ref[grows, :]
                m_old = m_ref[grows, :]
                m_new = jnp.maximum(m_old, jnp.max(u, axis=-1, keepdims=True) + bt)
                alpha = jnp.exp2(m_old - m_new)
                p = jnp.exp2(u - jnp.tile(m_new - bt, (1, reps)))
                l_ref[grows, :] = alpha * l_ref[grows, :] + jnp.sum(p, axis=-1, keepdims=True)
                m_ref[grows, :] = m_new
                acc_ref[grows, :] = alpha * acc_ref[grows, :]
                p_parts.append(p.astype(BF16))
        trows = slice(g0 * tq, (g0 + ng) * tq)
        acc_ref[trows, :] += jnp.dot(jnp.concatenate(p_parts, axis=0), v_ref[pl.ds(k0, tq), :],
                                     preferred_element_type=F32)

    def one_block(q0, kb, masked):
        k0 = pl.multiple_of(kb * tq, tq)
        for g0 in range(0, group, heads_per_dot):
            head_stack(q0, k0, g0, heads_per_dot, masked)

    def q_body(qi, carry):
        q0 = pl.multiple_of(qi * tq, tq)
        cblk = ccol_ref[pl.ds(q0, tq), :]
        lane = lax.broadcasted_iota(jnp.int32, cblk.shape, 1)
        for g in range(group):
            c_t = jnp.sum(jnp.where(lane == kvh * group + g, cblk, 0.0), axis=-1, keepdims=True)
            bt_ref[g * tq:(g + 1) * tq, :] = jnp.broadcast_to(c_t * LOG2_E, (tq, V7X_LANES))
        acc_ref[...] = jnp.zeros_like(acc_ref)
        l_ref[...] = jnp.zeros_like(l_ref)
        m_ref[...] = jnp.full_like(m_ref, MASK_VALUE)
        one_block(q0, qi, True)

        def kb_body(kb, c):
            one_block(q0, kb, False)
            return c

        lax.fori_loop(0, qi, kb_body, 0)
        out = acc_ref[...] / l_ref[...]
        for g in range(group):
            o_ref[pl.ds(q0, tq), g * HEAD_DIM:(g + 1) * HEAD_DIM] = (
                out[g * tq:(g + 1) * tq, :].astype(o_ref.dtype))
        return carry

    lax.fori_loop(0, seq // tq, q_body, 0)


def forgetting_attention(q, kv, c_col, c_row, n_kv_heads, *, tq_target=512):
    b, s, dq = q.shape
    assert HEAD_DIM == V7X_LANES
    group = dq // HEAD_DIM // n_kv_heads
    tq = _pick_tile(s, tq_target, V7X_LANES)
    cw = c_col.shape[2]
    kern = functools.partial(_fox_attn_kernel, tq=tq, group=group)
    stat = pltpu.VMEM((group * tq, V7X_LANES), F32)
    return pl.pallas_call(
        kern,
        grid=(b, n_kv_heads),
        in_specs=[
            pl.BlockSpec((None, s, group * HEAD_DIM), lambda bi, h: (bi, 0, h)),
            pl.BlockSpec((None, s, HEAD_DIM), lambda bi, h: (bi, 0, h)),
            pl.BlockSpec((None, s, HEAD_DIM), lambda bi, h: (bi, 0, n_kv_heads + h)),
            pl.BlockSpec((None, s, cw), lambda bi, h: (bi, 0, 0)),
            pl.BlockSpec((None, None, group, s), lambda bi, h: (bi, h, 0, 0)),
        ],
        out_specs=pl.BlockSpec((None, s, group * HEAD_DIM), lambda bi, h: (bi, 0, h)),
        out_shape=jax.ShapeDtypeStruct((b, s, dq), BF16),
        scratch_shapes=[stat, stat, stat, stat],
        compiler_params=_params("parallel", "parallel"),
        name="forgetting_attention",
    )(q, kv, kv, c_col, c_row)


def _rms_norm_kernel(h_ref, g_ref, o_ref):
    x = h_ref[...]
    ms = jnp.mean(x * x, axis=-1, keepdims=True)
    o_ref[...] = x * lax.rsqrt(ms + RMS_EPS) * g_ref[...]


def rms_norm(h, gain, *, tm_target=256):
    m, d = h.shape
    tm = _pick_tile(m, tm_target, V7X_SUBLANES)
    return pl.pallas_call(
        _rms_norm_kernel,
        grid=(m // tm,),
        in_specs=[pl.BlockSpec((tm, d), lambda i: (i, 0)),
                  pl.BlockSpec((1, d), lambda i: (0, 0))],
        out_specs=pl.BlockSpec((tm, d), lambda i: (i, 0)),
        out_shape=jax.ShapeDtypeStruct((m, d), F32),
        compiler_params=_params("parallel"),
        name="final_rms_norm",
    )(h, gain.reshape(1, d))


def _conv_glu_ffn(h, gain, w_up, w_conv, b_conv, w_down, seq):
    (xn,) = rms_norm_cast(h, [gain])
    act = ffn_up(xn, w_up, w_conv, b_conv, seq)
    return matmul_residual(act, w_down.astype(BF16), h, tm_target=512, tn_target=512)


def kernel(x, a_attn_norm, a_w_qkv, a_w_o, a_ffn_norm, a_w_up, a_w_conv, a_b_conv, a_w_down,
           kv_norm, w_kv, w_f, b_f,
           b_attn_norm, b_w_q, b_w_o, b_ffn_norm, b_w_up, b_w_conv, b_b_conv, b_w_down,
           final_norm):
    bsz, seq, d = x.shape
    m = bsz * seq
    n_heads = d // HEAD_DIM
    n_kv_heads = w_kv.shape[1] // (2 * HEAD_DIM)
    group = n_heads // n_kv_heads
    n_gates = w_f.shape[1]
    assert n_gates == n_heads and n_gates <= V7X_LANES

    h = x.reshape(m, d)

    for i in range(a_w_qkv.shape[0]):
        (xn,) = rms_norm_cast(h, [a_attn_norm[i]])
        qkv = matmul(xn, a_w_qkv[i].astype(BF16), scaled_cols=d, scale=SB_Q_SCALE)
        o = stick_breaking_attention(qkv.reshape(bsz, seq, 3 * d), n_heads)
        h = matmul_residual(o.reshape(m, d), a_w_o[i].astype(BF16), h, tm_target=1024, tn_target=1024)
        h = _conv_glu_ffn(h, a_ffn_norm[i], a_w_up[i], a_w_conv[i], a_b_conv[i], a_w_down[i], seq)

    kv = c_col = c_row = None
    for j in range(b_w_q.shape[0]):
        if j == 0:
            xn_kv, xn = rms_norm_cast(h, [kv_norm, b_attn_norm[j]])
            wf_pad = jnp.pad(w_f, ((0, 0), (0, V7X_LANES - n_gates))).astype(BF16)
            bf_pad = jnp.pad(b_f, (0, V7X_LANES - n_gates)).reshape(1, V7X_LANES)
            kv, log_f = kv_proj(xn_kv, w_kv.astype(BF16), wf_pad, bf_pad)
            kv = kv.reshape(bsz, seq, 2 * n_kv_heads * HEAD_DIM)
            c_col = cumsum_positions(log_f.reshape(bsz, seq, V7X_LANES))
            c_row = (c_col[:, :, :n_gates].transpose(0, 2, 1)
                     .reshape(bsz, n_kv_heads, group, seq))
        else:
            (xn,) = rms_norm_cast(h, [b_attn_norm[j]])
        q = matmul(xn, b_w_q[j].astype(BF16), scaled_cols=d, scale=FOX_Q_SCALE)
        o = forgetting_attention(q.reshape(bsz, seq, d), kv, c_col, c_row, n_kv_heads)
        h = matmul_residual(o.reshape(m, d), b_w_o[j].astype(BF16), h, tm_target=1024, tn_target=1024)
        h = _conv_glu_ffn(h, b_ffn_norm[j], b_w_up[j], b_w_conv[j], b_b_conv[j], b_w_down[j], seq)

    return rms_norm(h, final_norm).reshape(bsz, seq, d)
```
